```python
import jax, jax.numpy as jnp
from jax import lax
import numpy as np

D_MODEL = 2048
BATCH = 4
SEQ = 4096
DEPTH = 4

HEAD_DIM = 128
ROPE_THETA = 10000.0
BLOCK_Q = 128
EPS = 1e-6
NEG_INF = -1e30
A_HEADS = 8
A_KV_HEADS = 2
IDX_HEADS = 16
IDX_DIM = 128
IDX_TOPK = 256
CONV_CH = 1024
CONV_WIDTH = 31
C_HEADS = 8
N_BRANCH = 3

A_WIDTH = A_HEADS * HEAD_DIM
A_KV_WIDTH = A_KV_HEADS * HEAD_DIM
C_WIDTH = C_HEADS * HEAD_DIM
SPLITS = (A_WIDTH, A_KV_WIDTH, A_KV_WIDTH, A_WIDTH,
          IDX_HEADS * IDX_DIM, IDX_DIM, IDX_HEADS,
          2 * CONV_CH, CONV_CH,
          C_WIDTH, C_WIDTH, C_WIDTH, C_WIDTH,
          N_BRANCH * D_MODEL)
IN_WIDTH = sum(SPLITS)

kernel_name = "hybrid_dsa_conformer_stickbreak_block"


def rms_norm(x, w):
    xf = x.astype(jnp.float32)
    y = xf * lax.rsqrt(jnp.mean(xf * xf, axis=-1, keepdims=True) + EPS)
    return (y * w.astype(jnp.float32)).astype(x.dtype)


def layer_norm(x, w, b):
    xf = x.astype(jnp.float32)
    mu = jnp.mean(xf, axis=-1, keepdims=True)
    var = jnp.mean(jnp.square(xf - mu), axis=-1, keepdims=True)
    y = (xf - mu) * lax.rsqrt(var + EPS)
    return (y * w.astype(jnp.float32) + b.astype(jnp.float32)).astype(x.dtype)


def rope_tables(seq, dim):
    inv = 1.0 / (ROPE_THETA ** (jnp.arange(0, dim, 2, dtype=jnp.float32) / dim))
    ang = jnp.arange(seq, dtype=jnp.float32)[:, None] * inv[None, :]
    return jnp.cos(ang), jnp.sin(ang)


def apply_rope(x, cos, sin):
    half = x.shape[-1] // 2
    xf = x.astype(jnp.float32)
    x1, x2 = xf[..., :half], xf[..., half:]
    c, s = cos[None, :, None, :], sin[None, :, None, :]
    return jnp.concatenate([x1 * c - x2 * s, x2 * c + x1 * s], axis=-1).astype(x.dtype)


def split_points():
    return [int(o) for o in np.cumsum(np.array(SPLITS))[:-1]]


def dsa_attention(q, k, v, qi, ki, wi, top_k):
    b, s, h, dh = q.shape
    g = h // A_KV_HEADS
    gather = jax.vmap(lambda arr, ids: arr[ids])
    outs = []
    for start in range(0, s, BLOCK_Q):
        end = start + BLOCK_Q
        tq = start + jnp.arange(BLOCK_Q)
        causal = jnp.arange(end)[None, :] <= tq[:, None]
        rel = jnp.einsum('bthd,bsd->bhts', qi[:, start:end].astype(jnp.float32), ki[:, :end].astype(jnp.float32))
        score = jnp.einsum('bhts,bth->bts', jax.nn.relu(rel), wi[:, start:end].astype(jnp.float32))
        score = jnp.where(causal[None], score, NEG_INF)
        kk = min(top_k, end)
        _, idx = lax.top_k(score, kk)
        valid = idx <= tq[None, :, None]
        ks = gather(k, idx)
        vs = gather(v, idx)
        qb = q[:, start:end].reshape(b, BLOCK_Q, A_KV_HEADS, g, dh)
        logits = jnp.einsum('btcgd,btncd->btcgn', qb.astype(jnp.float32), ks.astype(jnp.float32)) * (dh ** -0.5)
        logits = jnp.where(valid[:, :, None, None, :], logits, NEG_INF)
        p = jax.nn.softmax(logits, axis=-1)
        o = jnp.einsum('btcgn,btncd->btcgd', p.astype(v.dtype), vs)
        outs.append(o.reshape(b, BLOCK_Q, h * dh))
    return jnp.concatenate(outs, axis=1)


def stick_breaking_attention(q, k, v):
    b, s, h, dh = q.shape
    outs = []
    for start in range(0, s, BLOCK_Q):
        end = start + BLOCK_Q
        tq = start + jnp.arange(BLOCK_Q)
        strict = (jnp.arange(end)[None, :] < tq[:, None])[None, None]
        z = jnp.einsum('bthd,bshd->bhts', q[:, start:end].astype(jnp.float32), k[:, :end].astype(jnp.float32)) * (dh ** -0.5)
        log_beta = jax.nn.log_sigmoid(z)
        log_keep = jnp.where(strict, jax.nn.log_sigmoid(-z), 0.0)
        after = lax.cumsum(log_keep, axis=3, reverse=True) - log_keep
        a = jnp.where(strict, jnp.exp(log_beta + after), 0.0)
        o = jnp.einsum('bhts,bshd->bthd', a.astype(v.dtype), v[:, :end])
        outs.append(o.reshape(b, BLOCK_Q, h * dh))
    return jnp.concatenate(outs, axis=1)


def conformer_conv(u, conv_w, conv_b, ln_w, ln_b):
    a, gt = jnp.split(u, 2, axis=-1)
    y = a * jax.nn.sigmoid(gt)
    y = lax.conv_general_dilated(y, conv_w.astype(y.dtype), window_strides=(1,),
                                 padding=[(CONV_WIDTH - 1, 0)],
                                 dimension_numbers=('NWC', 'WIO', 'NWC'),
                                 feature_group_count=CONV_CH) + conv_b
    y = layer_norm(y, ln_w, ln_b)
    return jax.nn.silu(y)


def hybrid_layer(x, norm_w, w_in, conv_w, conv_b, cln_w, cln_b, w_a_out, w_b_out, w_c_out, w_out, cos, sin, top_k):
    b, s, _ = x.shape
    h = rms_norm(x, norm_w)
    proj = h @ w_in
    (qa, ka, va, za, qi, ki, wi, ub, zb, qc, kc, vc, zc, gates) = jnp.split(proj, split_points(), axis=-1)
    qa = apply_rope(qa.reshape(b, s, A_HEADS, HEAD_DIM), cos, sin)
    ka = apply_rope(ka.reshape(b, s, A_KV_HEADS, HEAD_DIM), cos, sin)
    va = va.reshape(b, s, A_KV_HEADS, HEAD_DIM)
    qi = apply_rope(qi.reshape(b, s, IDX_HEADS, IDX_DIM), cos, sin)
    ki = apply_rope(ki.reshape(b, s, 1, IDX_DIM), cos, sin)[:, :, 0]
    wi = wi * (IDX_HEADS ** -0.5) * (IDX_DIM ** -0.5)
    y_a = (dsa_attention(qa, ka, va, qi, ki, wi, top_k) * jax.nn.silu(za)) @ w_a_out
    y_b = (conformer_conv(ub, conv_w, conv_b, cln_w, cln_b) * jax.nn.silu(zb)) @ w_b_out
    qc = qc.reshape(b, s, C_HEADS, HEAD_DIM)
    kc = kc.reshape(b, s, C_HEADS, HEAD_DIM)
    vc = vc.reshape(b, s, C_HEADS, HEAD_DIM)
    y_c = (stick_breaking_attention(qc, kc, vc) * jax.nn.silu(zc)) @ w_c_out
    g = jax.nn.sigmoid(gates).reshape(b, s, N_BRANCH, D_MODEL)
    merged = g[:, :, 0] * y_a + g[:, :, 1] * y_b + g[:, :, 2] * y_c
    return x + merged @ w_out


def setup_inputs(seed: int = 0) -> dict:
    key = jax.random.key(seed)
    ks = jax.random.split(key, 14)
    f32 = jnp.float32
    nrm = lambda k, shape, scale: jax.random.normal(k, shape, f32) * scale
    return {
        "x": nrm(ks[0], (BATCH, SEQ, D_MODEL), 1.0),
        "norm_w": 1.0 + nrm(ks[1], (DEPTH, D_MODEL), 0.02),
        "w_in": nrm(ks[2], (DEPTH, D_MODEL, IN_WIDTH), D_MODEL ** -0.5),
        "conv_w": nrm(ks[3], (DEPTH, CONV_WIDTH, 1, CONV_CH), CONV_WIDTH ** -0.5),
        "conv_b": nrm(ks[4], (DEPTH, CONV_CH), 0.01),
        "conv_ln_w": 1.0 + nrm(ks[5], (DEPTH, CONV_CH), 0.02),
        "conv_ln_b": nrm(ks[6], (DEPTH, CONV_CH), 0.01),
        "w_a_out": nrm(ks[7], (DEPTH, A_WIDTH, D_MODEL), A_WIDTH ** -0.5),
        "w_b_out": nrm(ks[8], (DEPTH, CONV_CH, D_MODEL), CONV_CH ** -0.5),
        "w_c_out": nrm(ks[9], (DEPTH, C_WIDTH, D_MODEL), C_WIDTH ** -0.5),
        "w_out": nrm(ks[10], (DEPTH, D_MODEL, D_MODEL), (2.0 * D_MODEL) ** -0.5),
        "final_norm_w": 1.0 + nrm(ks[11], (D_MODEL,), 0.02),
    }


def reference(x, norm_w, w_in, conv_w, conv_b, conv_ln_w, conv_ln_b, w_a_out, w_b_out, w_c_out, w_out, final_norm_w):
    seq = x.shape[1]
    top_k = min(IDX_TOPK, seq // 4)
    cos, sin = rope_tables(seq, HEAD_DIM)
    for i in range(DEPTH):
        x = hybrid_layer(x, norm_w[i], w_in[i], conv_w[i], conv_b[i], conv_ln_w[i], conv_ln_b[i],
                         w_a_out[i], w_b_out[i], w_c_out[i], w_out[i], cos, sin, top_k)
    return rms_norm(x, final_norm_w)
```

```python
import functools

import jax
import jax.numpy as jnp
from jax import lax
from jax.experimental import pallas as pl
from jax.experimental.pallas import tpu as pltpu

HEAD_DIM = 128
ROPE_THETA = 10000.0
BLOCK_Q = 128
EPS = 1e-6
NEG_INF = -1e30
MASK_BIAS = -2e30
A_HEADS = 8
A_KV_HEADS = 2
IDX_HEADS = 16
IDX_TOPK = 256
C_HEADS = 8
N_BRANCH = 3
LANES = 128
SUBLANES = 8
INT_MIN = -(2 ** 31)

MXU_DTYPE = jnp.bfloat16
F32 = jnp.float32

A_WIDTH = A_HEADS * HEAD_DIM
A_KV_WIDTH = A_KV_HEADS * HEAD_DIM
IDX_WIDTH = IDX_HEADS * HEAD_DIM
C_WIDTH = C_HEADS * HEAD_DIM

COL_TILE = 1024
COL_QI = 0
COL_QA = COL_QI + IDX_WIDTH
COL_QC = COL_QA + A_WIDTH
COL_KC = COL_QC + C_WIDTH
COL_VC = COL_KC + C_WIDTH
COL_MISC = COL_VC + C_WIDTH
MISC_KA = 0
MISC_VA = MISC_KA + A_KV_WIDTH
MISC_KI = MISC_VA + A_KV_WIDTH
MISC_WI = MISC_KI + HEAD_DIM
COL_ZA = COL_MISC + COL_TILE
COL_ZB = COL_ZA + A_WIDTH
COL_ZC = COL_ZB + COL_TILE
COL_UB = COL_ZC + C_WIDTH
COL_GATES = COL_UB + 2 * COL_TILE

VMEM_LIMIT = 56 * 1024 * 1024


def _nt_dot(a, b):
    return lax.dot_general(a, b, (((1,), (1,)), ((), ())), preferred_element_type=F32)


def _silu(x):
    return x * jax.nn.sigmoid(x)


def _inproj_kernel(x_ref, nw_ref, w_ref, o_ref, h_ref):
    @pl.when(pl.program_id(1) == 0)
    def _():
        xf = x_ref[...]
        ms = jnp.mean(xf * xf, axis=-1, keepdims=True)
        h_ref[...] = (xf * lax.rsqrt(ms + EPS) * nw_ref[...]).astype(h_ref.dtype)

    o_ref[...] = jnp.dot(h_ref[...], w_ref[...], preferred_element_type=F32)


def _inproj(x2d, norm_w, w_prep, tm, tn):
    n, d = x2d.shape
    np_ = w_prep.shape[1]
    return pl.pallas_call(
        _inproj_kernel,
        grid=(n // tm, np_ // tn),
        in_specs=[
            pl.BlockSpec((tm, d), lambda i, j: (i, 0)),
            pl.BlockSpec((1, d), lambda i, j: (0, 0)),
            pl.BlockSpec((d, tn), lambda i, j: (0, j)),
        ],
        out_specs=pl.BlockSpec((tm, tn), lambda i, j: (i, j)),
        out_shape=jax.ShapeDtypeStruct((n, np_), F32),
        scratch_shapes=[pltpu.VMEM((tm, d), MXU_DTYPE)],
        compiler_params=pltpu.CompilerParams(
            dimension_semantics=("arbitrary", "arbitrary"), vmem_limit_bytes=VMEM_LIMIT),
        name="inproj",
    )(x2d, norm_w.reshape(1, d), w_prep)


def _rope(x, cos, sin_signed):
    return x * cos + pltpu.roll(x, HEAD_DIM // 2, axis=1) * sin_signed


def _prep_kernel(qi0_ref, qi1_ref, qa_ref, qc_ref, kc_ref, vc_ref, misc_ref, cos_ref, sin_ref,
                 qi_o, qa_o, qc_o, kc_o, vc_o, ka_o, va_o, ki_o, wi_o, *, qk_scale, wi_scale):
    cos = cos_ref[...]
    sin = sin_ref[...]
    heads_per_tile = COL_TILE // HEAD_DIM
    for t, src in enumerate((qi0_ref, qi1_ref)):
        for h in range(heads_per_tile):
            x = src[:, h * HEAD_DIM:(h + 1) * HEAD_DIM]
            qi_o[0, t * heads_per_tile + h] = _rope(x, cos, sin).astype(qi_o.dtype)
    for h in range(A_HEADS):
        x = qa_ref[:, h * HEAD_DIM:(h + 1) * HEAD_DIM]
        qa_o[0, h] = (_rope(x, cos, sin) * qk_scale).astype(qa_o.dtype)
    for h in range(C_HEADS):
        sl = slice(h * HEAD_DIM, (h + 1) * HEAD_DIM)
        qc_o[0, h] = (qc_ref[:, sl] * qk_scale).astype(qc_o.dtype)
        kc_o[0, h] = kc_ref[:, sl].astype(kc_o.dtype)
        vc_o[0, h] = vc_ref[:, sl].astype(vc_o.dtype)
    for h in range(A_KV_HEADS):
        xk = misc_ref[:, MISC_KA + h * HEAD_DIM:MISC_KA + (h + 1) * HEAD_DIM]
        ka_o[0, h] = _rope(xk, cos, sin).astype(ka_o.dtype)
        va_o[0, h] = misc_ref[:, MISC_VA + h * HEAD_DIM:MISC_VA + (h + 1) * HEAD_DIM].astype(va_o.dtype)
    ki_o[0] = _rope(misc_ref[:, MISC_KI:MISC_KI + HEAD_DIM], cos, sin).astype(ki_o.dtype)
    wi_o[...] = misc_ref[:, MISC_WI:MISC_WI + LANES] * wi_scale[0] * wi_scale[1]


def _prep(proj, cos_t, sin_t, batch, seq, ts):
    n = proj.shape[0]
    nst = seq // ts
    row = lambda b, s: b * nst + s
    col = lambda c: pl.BlockSpec((ts, COL_TILE), lambda b, s, c=c: (row(b, s), c))
    tab = pl.BlockSpec((ts, HEAD_DIM), lambda b, s: (s, 0))
    heads = lambda nh: pl.BlockSpec((1, nh, ts, HEAD_DIM), lambda b, s: (b, 0, s, 0))
    hshape = lambda nh: jax.ShapeDtypeStruct((batch, nh, seq, HEAD_DIM), MXU_DTYPE)
    kern = functools.partial(
        _prep_kernel, qk_scale=HEAD_DIM ** -0.5, wi_scale=(IDX_HEADS ** -0.5, HEAD_DIM ** -0.5))
    return pl.pallas_call(
        kern,
        grid=(batch, nst),
        in_specs=[col(c) for c in range(COL_MISC // COL_TILE + 1)] + [tab, tab],
        out_specs=[
            heads(IDX_HEADS), heads(A_HEADS), heads(C_HEADS), heads(C_HEADS), heads(C_HEADS),
            heads(A_KV_HEADS), heads(A_KV_HEADS),
            pl.BlockSpec((1, ts, HEAD_DIM), lambda b, s: (b, s, 0)),
            pl.BlockSpec((ts, LANES), lambda b, s: (row(b, s), 0)),
        ],
        out_shape=[
            hshape(IDX_HEADS), hshape(A_HEADS), hshape(C_HEADS), hshape(C_HEADS), hshape(C_HEADS),
            hshape(A_KV_HEADS), hshape(A_KV_HEADS),
            jax.ShapeDtypeStruct((batch, seq, HEAD_DIM), MXU_DTYPE),
            jax.ShapeDtypeStruct((n, LANES), F32),
        ],
        compiler_params=pltpu.CompilerParams(
            dimension_semantics=("arbitrary", "arbitrary"), vmem_limit_bytes=VMEM_LIMIT),
        name="prep",
    )(*([proj] * (COL_MISC // COL_TILE + 1)), cos_t, sin_t)


def _sortable_key(x):
    bits = lax.bitcast_convert_type(x, jnp.int32)
    return jnp.where(bits < 0, bits ^ jnp.int32(0x7FFFFFFF), bits)


def _dsa_kernel(qi_ref, ki_ref, wi_ref, qa_ref, ka_ref, va_ref, za_ref, o_ref,
                key_ref, wb_ref, thr_ref, *, top_k, kc):
    i = pl.program_id(1)
    end = (i + 1) * BLOCK_Q
    nch = (end + kc - 1) // kc
    t_idx = i * BLOCK_Q + lax.broadcasted_iota(jnp.int32, (BLOCK_Q, kc), 0)
    lane = lax.broadcasted_iota(jnp.int32, (BLOCK_Q, kc), 1)

    wi = wi_ref[...]
    for h in range(IDX_HEADS):
        wb_ref[h] = jnp.broadcast_to(wi[:, h:h + 1], (BLOCK_Q, kc))
    qi = qi_ref[0].reshape(IDX_HEADS * BLOCK_Q, HEAD_DIM)

    def score_body(c, carry):
        k0 = pl.multiple_of(c * kc, kc)
        rel = _nt_dot(qi, ki_ref[0, pl.ds(k0, kc), :])
        acc = wb_ref[0] * jnp.maximum(rel[0:BLOCK_Q], 0.0)
        for h in range(1, IDX_HEADS):
            acc = acc + wb_ref[h] * jnp.maximum(rel[h * BLOCK_Q:(h + 1) * BLOCK_Q], 0.0)
        score = jnp.where(k0 + lane <= t_idx, acc, NEG_INF)
        key_ref[:, pl.ds(k0, kc)] = _sortable_key(score)
        return carry

    lax.fori_loop(0, nch, score_body, 0)

    select_all = end <= top_k

    @pl.when(select_all)
    def _():
        thr_ref[...] = jnp.full(thr_ref.shape, INT_MIN, jnp.int32)

    @pl.when(jnp.logical_not(select_all))
    def _():
        def bit_body(b, u):
            bit = lax.shift_left(jnp.int32(1), 31 - b)
            cand = (u | bit) ^ jnp.int32(INT_MIN)

            def cnt_body(c, cnt):
                kk = key_ref[:, pl.ds(pl.multiple_of(c * LANES, LANES), LANES)]
                return cnt + jnp.where(kk >= cand, 1.0, 0.0)

            cnt = lax.fori_loop(0, i + 1, cnt_body, jnp.zeros((BLOCK_Q, LANES), F32))
            tot = jnp.sum(cnt, axis=1, keepdims=True)
            return jnp.where(tot >= float(top_k), u | bit, u)

        u = lax.fori_loop(0, 32, bit_body, jnp.zeros((BLOCK_Q, LANES), jnp.int32))
        thr_ref[...] = u ^ jnp.int32(INT_MIN)

    thr = thr_ref[...]
    thr_k = jnp.concatenate([thr] * (kc // LANES), axis=1)
    group = A_HEADS // A_KV_HEADS
    rows = group * BLOCK_Q
    for c2 in range(A_KV_HEADS):
        q = qa_ref[0, c2 * group:(c2 + 1) * group].reshape(rows, HEAD_DIM)

        def att_body(c, carry, c2=c2, q=q):
            m, l, acc = carry
            k0 = pl.multiple_of(c * kc, kc)
            kk = key_ref[:, pl.ds(k0, kc)]
            bias = jnp.where(k0 + lane <= t_idx, jnp.where(kk >= thr_k, 0.0, MASK_BIAS), MASK_BIAS)
            s = _nt_dot(q, ka_ref[0, c2, pl.ds(k0, kc), :])
            s = s + jnp.concatenate([bias] * group, axis=0)
            m_new = jnp.maximum(m, jnp.max(s, axis=1, keepdims=True))
            alpha = jnp.exp(m - m_new)
            p = jnp.exp(s - m_new)
            l_new = alpha * l + jnp.sum(p, axis=1, keepdims=True)
            pv = jnp.dot(p.astype(MXU_DTYPE), va_ref[0, c2, pl.ds(k0, kc), :],
                         preferred_element_type=F32)
            return m_new, l_new, alpha * acc + pv

        init = (jnp.full((rows, 1), NEG_INF, F32), jnp.zeros((rows, 1), F32),
                jnp.zeros((rows, HEAD_DIM), F32))
        _, l, acc = lax.fori_loop(0, nch, att_body, init)
        out = acc / l
        for g in range(group):
            sl = slice((c2 * group + g) * HEAD_DIM, (c2 * group + g + 1) * HEAD_DIM)
            o_ref[:, sl] = (out[g * BLOCK_Q:(g + 1) * BLOCK_Q] * _silu(za_ref[:, sl])).astype(o_ref.dtype)


def _dsa(qi, ki, wi, qa, ka, va, proj, batch, seq, top_k, kc):
    nqb = seq // BLOCK_Q
    n = batch * seq
    row = lambda b, i: b * nqb + i
    kern = functools.partial(_dsa_kernel, top_k=top_k, kc=kc)
    return pl.pallas_call(
        kern,
        grid=(batch, nqb),
        in_specs=[
            pl.BlockSpec((1, IDX_HEADS, BLOCK_Q, HEAD_DIM), lambda b, i: (b, 0, i, 0)),
            pl.BlockSpec((1, seq, HEAD_DIM), lambda b, i: (b, 0, 0)),
            pl.BlockSpec((BLOCK_Q, LANES), lambda b, i: (row(b, i), 0)),
            pl.BlockSpec((1, A_HEADS, BLOCK_Q, HEAD_DIM), lambda b, i: (b, 0, i, 0)),
            pl.BlockSpec((1, A_KV_HEADS, seq, HEAD_DIM), lambda b, i: (b, 0, 0, 0)),
            pl.BlockSpec((1, A_KV_HEADS, seq, HEAD_DIM), lambda b, i: (b, 0, 0, 0)),
            pl.BlockSpec((BLOCK_Q, A_WIDTH), lambda b, i: (row(b, i), COL_ZA // A_WIDTH)),
        ],
        out_specs=pl.BlockSpec((BLOCK_Q, A_WIDTH), lambda b, i: (row(b, i), 0)),
        out_shape=jax.ShapeDtypeStruct((n, A_WIDTH), MXU_DTYPE),
        scratch_shapes=[
            pltpu.VMEM((BLOCK_Q, seq + kc), jnp.int32),
            pltpu.VMEM((IDX_HEADS, BLOCK_Q, kc), F32),
            pltpu.VMEM((BLOCK_Q, LANES), jnp.int32),
        ],
        compiler_params=pltpu.CompilerParams(
            dimension_semantics=("arbitrary", "arbitrary"), vmem_limit_bytes=VMEM_LIMIT),
        name="dsa",
    )(qi, ki, wi, qa, ka, va, proj)


def _sb_kernel(q_ref, k_ref, v_ref, z_ref, o_ref, *, kc):
    i = pl.program_id(2)
    end = (i + 1) * BLOCK_Q
    nch = (end + kc - 1) // kc
    t_idx = i * BLOCK_Q + lax.broadcasted_iota(jnp.int32, (BLOCK_Q, kc), 0)
    lane = lax.broadcasted_iota(jnp.int32, (BLOCK_Q, kc), 1)
    tri = jnp.where(lax.broadcasted_iota(jnp.int32, (kc, kc), 0) > lax.broadcasted_iota(jnp.int32, (kc, kc), 1),
                    1.0, 0.0).astype(MXU_DTYPE)
    q = q_ref[0, 0]

    def body(j, carry):
        rest, acc = carry
        k0 = pl.multiple_of((nch - 1 - j) * kc, kc)
        z = _nt_dot(q, k_ref[0, 0, pl.ds(k0, kc), :])
        strict = k0 + lane < t_idx
        sp = jnp.log1p(jnp.exp(-jnp.abs(z)))
        log_beta = jnp.minimum(z, 0.0) - sp
        log_keep = jnp.where(strict, -jnp.maximum(z, 0.0) - sp, 0.0)
        hi = log_keep.astype(MXU_DTYPE)
        lo = (log_keep - hi.astype(F32)).astype(MXU_DTYPE)
        after = (jnp.dot(hi, tri, preferred_element_type=F32)
                 + jnp.dot(lo, tri, preferred_element_type=F32))
        a = jnp.where(strict, jnp.exp(log_beta + after + rest), 0.0)
        acc = acc + jnp.dot(a.astype(MXU_DTYPE), v_ref[0, 0, pl.ds(k0, kc), :],
                            preferred_element_type=F32)
        return rest + jnp.sum(log_keep, axis=1, keepdims=True), acc

    init = (jnp.zeros((BLOCK_Q, 1), F32), jnp.zeros((BLOCK_Q, HEAD_DIM), F32))
    _, acc = lax.fori_loop(0, nch, body, init)
    o_ref[...] = (acc * _silu(z_ref[...])).astype(o_ref.dtype)


def _sb(qc, kc_, vc, proj, batch, seq, kc):
    nqb = seq // BLOCK_Q
    n = batch * seq
    col_zc = COL_ZC // HEAD_DIM
    kern = functools.partial(_sb_kernel, kc=kc)
    return pl.pallas_call(
        kern,
        grid=(batch, C_HEADS, nqb),
        in_specs=[
            pl.BlockSpec((1, 1, BLOCK_Q, HEAD_DIM), lambda b, h, i: (b, h, i, 0)),
            pl.BlockSpec((1, 1, seq, HEAD_DIM), lambda b, h, i: (b, h, 0, 0)),
            pl.BlockSpec((1, 1, seq, HEAD_DIM), lambda b, h, i: (b, h, 0, 0)),
            pl.BlockSpec((BLOCK_Q, HEAD_DIM), lambda b, h, i: (b * nqb + i, col_zc + h)),
        ],
        out_specs=pl.BlockSpec((BLOCK_Q, HEAD_DIM), lambda b, h, i: (b * nqb + i, h)),
        out_shape=jax.ShapeDtypeStruct((n, C_WIDTH), MXU_DTYPE),
        compiler_params=pltpu.CompilerParams(
            dimension_semantics=("arbitrary", "arbitrary", "arbitrary"), vmem_limit_bytes=VMEM_LIMIT),
        name="sb",
    )(qc, kc_, vc, proj)


CONV_HIST = 32


def _conv_kernel(u_ref, zb_ref, cw_ref, cb_ref, lw_ref, lb_ref, o_ref, ysh, *, ts, width, rc):
    ch = o_ref.shape[1]

    @pl.when(pl.program_id(1) == 0)
    def _():
        ysh[0, 0:CONV_HIST, :] = jnp.zeros((CONV_HIST, ch), F32)

    @pl.when(pl.program_id(1) > 0)
    def _():
        ysh[0, 0:CONV_HIST, :] = ysh[0, ts:ts + CONV_HIST, :]

    ysh[0, CONV_HIST:CONV_HIST + ts, :] = u_ref[:, 0:ch] * jax.nn.sigmoid(u_ref[:, ch:2 * ch])
    span = CONV_HIST + ts - SUBLANES
    for r in range(1, SUBLANES):
        ysh[r, 0:span, :] = ysh[0, r:r + span, :]

    def chunk(c, carry):
        r0 = pl.multiple_of(c * rc, rc)
        acc = jnp.broadcast_to(cb_ref[...], (rc, ch))
        for j in range(width):
            off = CONV_HIST - (width - 1) + j
            r = off % SUBLANES
            acc = acc + cw_ref[j:j + 1, :] * ysh[r, pl.ds(r0 + (off - r), rc), :]
        mu = jnp.mean(acc, axis=1, keepdims=True)
        d = acc - mu
        var = jnp.mean(d * d, axis=1, keepdims=True)
        y = d * lax.rsqrt(var + EPS) * lw_ref[...] + lb_ref[...]
        o_ref[pl.ds(r0, rc), :] = (_silu(y) * _silu(zb_ref[pl.ds(r0, rc), :])).astype(o_ref.dtype)
        return carry

    lax.fori_loop(0, ts // rc, chunk, 0)


def _conv(proj, conv_w, conv_b, ln_w, ln_b, batch, seq, ts, rc):
    n = batch * seq
    width, ch = conv_w.shape
    nst = seq // ts
    col_ub = COL_UB // (2 * ch)
    vec = pl.BlockSpec((1, ch), lambda b, s: (0, 0))
    kern = functools.partial(_conv_kernel, ts=ts, width=width, rc=rc)
    return pl.pallas_call(
        kern,
        grid=(batch, nst),
        in_specs=[
            pl.BlockSpec((ts, 2 * ch), lambda b, s: (b * nst + s, col_ub)),
            pl.BlockSpec((ts, ch), lambda b, s: (b * nst + s, COL_ZB // ch)),
            pl.BlockSpec((width, ch), lambda b, s: (0, 0)),
            vec, vec, vec,
        ],
        out_specs=pl.BlockSpec((ts, ch), lambda b, s: (b * nst + s, 0)),
        out_shape=jax.ShapeDtypeStruct((n, ch), MXU_DTYPE),
        scratch_shapes=[pltpu.VMEM((SUBLANES, CONV_HIST + ts, ch), F32)],
        compiler_params=pltpu.CompilerParams(
            dimension_semantics=("arbitrary", "arbitrary"), vmem_limit_bytes=VMEM_LIMIT),
        name="conv",
    )(proj, proj, conv_w, conv_b.reshape(1, ch), ln_w.reshape(1, ch), ln_b.reshape(1, ch))


def _merge_kernel(ya_ref, yb_ref, yc_ref, g0_ref, g1_ref, g2_ref, wa_ref, wb_ref, wc_ref, o_ref):
    ya = jnp.dot(ya_ref[...], wa_ref[...], preferred_element_type=F32)
    yb = jnp.dot(yb_ref[...], wb_ref[...], preferred_element_type=F32)
    yc = jnp.dot(yc_ref[...], wc_ref[...], preferred_element_type=F32)
    merged = (jax.nn.sigmoid(g0_ref[...]) * ya + jax.nn.sigmoid(g1_ref[...]) * yb
              + jax.nn.sigmoid(g2_ref[...]) * yc)
    o_ref[...] = merged.astype(o_ref.dtype)


def _merge(ya, yb, yc, proj, wa, wb, wc, col_gates, tm):
    n = ya.shape[0]
    d = wa.shape[1]
    act = lambda w: pl.BlockSpec((tm, w), lambda i: (i, 0))
    gate = lambda k: pl.BlockSpec((tm, d), lambda i, k=k: (i, col_gates // d + k))
    wspec = lambda w: pl.BlockSpec(w.shape, lambda i: (0, 0))
    return pl.pallas_call(
        _merge_kernel,
        grid=(n // tm,),
        in_specs=[act(ya.shape[1]), act(yb.shape[1]), act(yc.shape[1]), gate(0), gate(1), gate(2),
                  wspec(wa), wspec(wb), wspec(wc)],
        out_specs=pl.BlockSpec((tm, d), lambda i: (i, 0)),
        out_shape=jax.ShapeDtypeStruct((n, d), MXU_DTYPE),
        compiler_params=pltpu.CompilerParams(
            dimension_semantics=("arbitrary",), vmem_limit_bytes=VMEM_LIMIT),
        name="merge",
    )(ya, yb, yc, proj, proj, proj, wa, wb, wc)


def _outproj_kernel(m_ref, x_ref, w_ref, fw_ref, o_ref, *, final_norm):
    y = x_ref[...] + jnp.dot(m_ref[...], w_ref[...], preferred_element_type=F32)
    if final_norm:
        ms = jnp.mean(y * y, axis=-1, keepdims=True)
        y = y * lax.rsqrt(ms + EPS) * fw_ref[...]
    o_ref[...] = y


def _outproj(merged, x2d, w_out, final_w, final_norm, tm):
    n, d = x2d.shape
    kern = functools.partial(_outproj_kernel, final_norm=final_norm)
    return pl.pallas_call(
        kern,
        grid=(n // tm,),
        in_specs=[
            pl.BlockSpec((tm, d), lambda i: (i, 0)),
            pl.BlockSpec((tm, d), lambda i: (i, 0)),
            pl.BlockSpec((d, d), lambda i: (0, 0)),
            pl.BlockSpec((1, d), lambda i: (0, 0)),
        ],
        out_specs=pl.BlockSpec((tm, d), lambda i: (i, 0)),
        out_shape=jax.ShapeDtypeStruct((n, d), F32),
        compiler_params=pltpu.CompilerParams(
            dimension_semantics=("arbitrary",), vmem_limit_bytes=VMEM_LIMIT),
        name="outproj",
    )(merged, x2d, w_out, final_w.reshape(1, d))


def _prepare_w_in(w_in, d, conv_ch):
    sizes = (A_WIDTH, A_KV_WIDTH, A_KV_WIDTH, A_WIDTH, IDX_WIDTH, HEAD_DIM, IDX_HEADS,
             2 * conv_ch, conv_ch, C_WIDTH, C_WIDTH, C_WIDTH, C_WIDTH, N_BRANCH * d)
    offs = [0]
    for s in sizes:
        offs.append(offs[-1] + s)
    assert offs[-1] == w_in.shape[-1]
    piece = lambda k: w_in[..., offs[k]:offs[k + 1]]
    qa, ka, va, za, qi, ki, wi, ub, zb, qc, kc, vc, zc, gates = (piece(k) for k in range(len(sizes)))
    zeros = lambda w: jnp.zeros(w_in.shape[:-1] + (w,), w_in.dtype)
    misc_used = 2 * A_KV_WIDTH + HEAD_DIM + LANES
    cols = [qi, qa, qc, kc, vc,
            ka, va, ki, wi, zeros(LANES - IDX_HEADS), zeros(COL_TILE - misc_used),
            za, zb, zc, ub, gates]
    used = COL_GATES + N_BRANCH * d
    np_ = -(-used // COL_TILE) * COL_TILE
    if np_ > used:
        cols.append(zeros(np_ - used))
    return jnp.concatenate(cols, axis=-1).astype(MXU_DTYPE)


def _rope_tables(seq):
    inv = 1.0 / (ROPE_THETA ** (jnp.arange(0, HEAD_DIM, 2, dtype=F32) / HEAD_DIM))
    ang = jnp.arange(seq, dtype=F32)[:, None] * inv[None, :]
    cos, sin = jnp.cos(ang), jnp.sin(ang)
    return jnp.concatenate([cos, cos], axis=1), jnp.concatenate([-sin, sin], axis=1)


def kernel(x, norm_w, w_in, conv_w, conv_b, conv_ln_w, conv_ln_b, w_a_out, w_b_out, w_c_out, w_out, final_norm_w):
    batch, seq, d = x.shape
    depth = norm_w.shape[0]
    conv_ch = conv_b.shape[-1]
    n = batch * seq
    assert seq % BLOCK_Q == 0 and conv_ch == COL_TILE and conv_w.shape[1] - 1 <= CONV_HIST
    top_k = min(IDX_TOPK, seq // 4)
    kc = 256
    assert seq % kc == 0 and COL_GATES % d == 0

    w_prep = _prepare_w_in(w_in, d, conv_ch)
    wa, wb, wc, wo = (w.astype(MXU_DTYPE) for w in (w_a_out, w_b_out, w_c_out, w_out))
    cw = conv_w.reshape(depth, conv_w.shape[1], conv_ch)
    cos_t, sin_t = _rope_tables(seq)

    tm_in = min(1024, n)
    ts_prep = min(256, seq)
    ts_conv = min(512, seq)

    x2d = x.reshape(n, d)
    for layer in range(depth):
        proj = _inproj(x2d, norm_w[layer], w_prep[layer], tm_in, COL_TILE)
        qi, qa, qc, kcs, vcs, ka, va, ki, wi = _prep(proj, cos_t, sin_t, batch, seq, ts_prep)
        ya = _dsa(qi, ki, wi, qa, ka, va, proj, batch, seq, top_k, kc)
        yc = _sb(qc, kcs, vcs, proj, batch, seq, kc)
        yb = _conv(proj, cw[layer], conv_b[layer], conv_ln_w[layer], conv_ln_b[layer], batch, seq, ts_conv, 32)
        merged = _merge(ya, yb, yc, proj, wa[layer], wb[layer], wc[layer], COL_GATES, min(256, n))
        x2d = _outproj(merged, x2d, wo[layer], final_norm_w, layer == depth - 1, min(512, n))
    return x2d.reshape(batch, seq, d)
```

```python
import functools

import jax
import jax.numpy as jnp
from jax import lax
from jax.experimental import pallas as pl
from jax.experimental.pallas import tpu as pltpu

HEAD_DIM = 128
ROPE_THETA = 10000.0
BLOCK_Q = 128
EPS = 1e-6
NEG_INF = -1e30
MASK_BIAS = -2e30
A_HEADS = 8
A_KV_HEADS = 2
IDX_HEADS = 16
IDX_TOPK = 256
C_HEADS = 8
N_BRANCH = 3
LANES = 128
SUBLANES = 8
LOG2E = 1.4426950408889634
INT_MIN = -(2 ** 31)

MXU_DTYPE = jnp.bfloat16
F32 = jnp.float32

A_WIDTH = A_HEADS * HEAD_DIM
A_KV_WIDTH = A_KV_HEADS * HEAD_DIM
IDX_WIDTH = IDX_HEADS * HEAD_DIM
C_WIDTH = C_HEADS * HEAD_DIM

COL_TILE = 1024
COL_QI = 0
COL_QA = COL_QI + IDX_WIDTH
COL_QC = COL_QA + A_WIDTH
COL_KC = COL_QC + C_WIDTH
COL_VC = COL_KC + C_WIDTH
COL_MISC = COL_VC + C_WIDTH
MISC_KA = 0
MISC_VA = MISC_KA + A_KV_WIDTH
MISC_KI = MISC_VA + A_KV_WIDTH
MISC_WI = MISC_KI + HEAD_DIM
COL_ZA = COL_MISC + COL_TILE
COL_ZB = COL_ZA + A_WIDTH
COL_ZC = COL_ZB + COL_TILE
COL_UB = COL_ZC + C_WIDTH
COL_GATES = COL_UB + 2 * COL_TILE

VMEM_LIMIT = 56 * 1024 * 1024


def _nt_dot(a, b):
    return lax.dot_general(a, b, (((1,), (1,)), ((), ())), preferred_element_type=F32)


def _silu(x):
    return x * jax.nn.sigmoid(x)


def _inproj_kernel(x_ref, nw_ref, w_ref, o_ref, h_ref):
    @pl.when(pl.program_id(1) == 0)
    def _():
        xf = x_ref[...]
        ms = jnp.mean(xf * xf, axis=-1, keepdims=True)
        h_ref[...] = (xf * lax.rsqrt(ms + EPS) * nw_ref[...]).astype(h_ref.dtype)

    o_ref[...] = jnp.dot(h_ref[...], w_ref[...], preferred_element_type=F32)


def _inproj(x2d, norm_w, w_prep, tm, tn):
    n, d = x2d.shape
    np_ = w_prep.shape[1]
    return pl.pallas_call(
        _inproj_kernel,
        grid=(n // tm, np_ // tn),
        in_specs=[
            pl.BlockSpec((tm, d), lambda i, j: (i, 0)),
            pl.BlockSpec((1, d), lambda i, j: (0, 0)),
            pl.BlockSpec((d, tn), lambda i, j: (0, j)),
        ],
        out_specs=pl.BlockSpec((tm, tn), lambda i, j: (i, j)),
        out_shape=jax.ShapeDtypeStruct((n, np_), F32),
        scratch_shapes=[pltpu.VMEM((tm, d), MXU_DTYPE)],
        compiler_params=pltpu.CompilerParams(
            dimension_semantics=("arbitrary", "arbitrary"), vmem_limit_bytes=VMEM_LIMIT),
        name="inproj",
    )(x2d, norm_w.reshape(1, d), w_prep)


def _rope(x, cos, sin_signed):
    return x * cos + pltpu.roll(x, HEAD_DIM // 2, axis=1) * sin_signed


def _prep_kernel(qi0_ref, qi1_ref, qa_ref, qc_ref, kc_ref, vc_ref, misc_ref, cos_ref, sin_ref,
                 qi_o, qa_o, qc_o, kc_o, vc_o, ka_o, va_o, ki_o, wi_o, *, qk_scale, wi_scale):
    cos = cos_ref[...]
    sin = sin_ref[...]
    heads_per_tile = COL_TILE // HEAD_DIM
    for t, src in enumerate((qi0_ref, qi1_ref)):
        for h in range(heads_per_tile):
            x = src[:, h * HEAD_DIM:(h + 1) * HEAD_DIM]
            qi_o[0, t * heads_per_tile + h] = _rope(x, cos, sin).astype(qi_o.dtype)
    for h in range(A_HEADS):
        x = qa_ref[:, h * HEAD_DIM:(h + 1) * HEAD_DIM]
        qa_o[0, h] = (_rope(x, cos, sin) * qk_scale).astype(qa_o.dtype)
    for h in range(C_HEADS):
        sl = slice(h * HEAD_DIM, (h + 1) * HEAD_DIM)
        qc_o[0, h] = (qc_ref[:, sl] * qk_scale).astype(qc_o.dtype)
        kc_o[0, h] = kc_ref[:, sl].astype(kc_o.dtype)
        vc_o[0, h] = vc_ref[:, sl].astype(vc_o.dtype)
    for h in range(A_KV_HEADS):
        xk = misc_ref[:, MISC_KA + h * HEAD_DIM:MISC_KA + (h + 1) * HEAD_DIM]
        ka_o[0, h] = _rope(xk, cos, sin).astype(ka_o.dtype)
        va_o[0, h] = misc_ref[:, MISC_VA + h * HEAD_DIM:MISC_VA + (h + 1) * HEAD_DIM].astype(va_o.dtype)
    ki_o[0] = _rope(misc_ref[:, MISC_KI:MISC_KI + HEAD_DIM], cos, sin).astype(ki_o.dtype)
    wi_o[...] = misc_ref[:, MISC_WI:MISC_WI + LANES] * wi_scale[0] * wi_scale[1]


def _prep(proj, cos_t, sin_t, batch, seq, ts):
    n = proj.shape[0]
    nst = seq // ts
    row = lambda b, s: b * nst + s
    col = lambda c: pl.BlockSpec((ts, COL_TILE), lambda b, s, c=c: (row(b, s), c))
    tab = pl.BlockSpec((ts, HEAD_DIM), lambda b, s: (s, 0))
    heads = lambda nh: pl.BlockSpec((1, nh, ts, HEAD_DIM), lambda b, s: (b, 0, s, 0))
    hshape = lambda nh: jax.ShapeDtypeStruct((batch, nh, seq, HEAD_DIM), MXU_DTYPE)
    kern = functools.partial(
        _prep_kernel, qk_scale=HEAD_DIM ** -0.5 * LOG2E, wi_scale=(IDX_HEADS ** -0.5, HEAD_DIM ** -0.5))
    return pl.pallas_call(
        kern,
        grid=(batch, nst),
        in_specs=[col(c) for c in range(COL_MISC // COL_TILE + 1)] + [tab, tab],
        out_specs=[
            heads(IDX_HEADS), heads(A_HEADS), heads(C_HEADS), heads(C_HEADS), heads(C_HEADS),
            heads(A_KV_HEADS), heads(A_KV_HEADS),
            pl.BlockSpec((1, ts, HEAD_DIM), lambda b, s: (b, s, 0)),
            pl.BlockSpec((ts, LANES), lambda b, s: (row(b, s), 0)),
        ],
        out_shape=[
            hshape(IDX_HEADS), hshape(A_HEADS), hshape(C_HEADS), hshape(C_HEADS), hshape(C_HEADS),
            hshape(A_KV_HEADS), hshape(A_KV_HEADS),
            jax.ShapeDtypeStruct((batch, seq, HEAD_DIM), MXU_DTYPE),
            jax.ShapeDtypeStruct((n, LANES), F32),
        ],
        compiler_params=pltpu.CompilerParams(
            dimension_semantics=("arbitrary", "arbitrary"), vmem_limit_bytes=VMEM_LIMIT),
        name="prep",
    )(*([proj] * (COL_MISC // COL_TILE + 1)), cos_t, sin_t)


def _sortable_key(x):
    bits = lax.bitcast_convert_type(x, jnp.int32)
    return jnp.where(bits < 0, bits ^ jnp.int32(0x7FFFFFFF), bits)


IDX_HEAD_GROUP = 4


def _dsa_kernel(qi_ref, ki_ref, wi_ref, qa_ref, ka_ref, va_ref, za_ref, o_ref,
                key_ref, wb_ref, thr_ref, m_ref, l_ref, acc_ref, *, top_k, tq, kc):
    i = pl.program_id(1)
    nch = ((i + 1) * tq) // kc
    t_idx = i * tq + lax.broadcasted_iota(jnp.int32, (tq, kc), 0)
    lane = lax.broadcasted_iota(jnp.int32, (tq, kc), 1)
    rep = kc // LANES

    wi = wi_ref[...]
    for h in range(IDX_HEADS):
        wb_ref[h] = jnp.broadcast_to(wi[:, h:h + 1], (tq, kc))

    def score_body(c, carry):
        k0 = pl.multiple_of(c * kc, kc)
        kch = ki_ref[0, pl.ds(k0, kc), :]
        acc = None
        for g in range(IDX_HEADS // IDX_HEAD_GROUP):
            qg = qi_ref[0, g * IDX_HEAD_GROUP:(g + 1) * IDX_HEAD_GROUP].reshape(IDX_HEAD_GROUP * tq, HEAD_DIM)
            rel = _nt_dot(qg, kch)
            for hh in range(IDX_HEAD_GROUP):
                term = wb_ref[g * IDX_HEAD_GROUP + hh] * jnp.maximum(rel[hh * tq:(hh + 1) * tq], 0.0)
                acc = term if acc is None else acc + term
        score = jnp.where(k0 + lane <= t_idx, acc, NEG_INF)
        key_ref[:, pl.ds(k0, kc)] = _sortable_key(score)
        return carry

    lax.fori_loop(0, nch, score_body, 0)

    need_bisect = (i + 1) * tq > top_k

    @pl.when(jnp.logical_not(need_bisect))
    def _():
        thr_ref[...] = jnp.full(thr_ref.shape, INT_MIN, jnp.int32)

    @pl.when(need_bisect)
    def _():
        def bit_body(b, u):
            bit = lax.shift_left(jnp.int32(1), 31 - b)
            cand = (u | bit) ^ jnp.int32(INT_MIN)
            cand_k = jnp.concatenate([cand] * rep, axis=1)

            def cnt_body(c, cnt):
                kk = key_ref[:, pl.ds(pl.multiple_of(c * kc, kc), kc)]
                ind = jnp.where(kk >= cand_k, 1.0, 0.0)
                for r in range(rep):
                    cnt = cnt + ind[:, r * LANES:(r + 1) * LANES]
                return cnt

            cnt = lax.fori_loop(0, nch, cnt_body, jnp.zeros((tq, LANES), F32))
            tot = jnp.sum(cnt, axis=1, keepdims=True)
            return jnp.where(tot >= float(top_k), u | bit, u)

        u = lax.fori_loop(0, 32, bit_body, jnp.zeros((tq, LANES), jnp.int32))
        t_row = i * tq + lax.broadcasted_iota(jnp.int32, (tq, LANES), 0)
        thr_ref[...] = jnp.where(t_row < top_k, jnp.int32(INT_MIN), u ^ jnp.int32(INT_MIN))

    thr_k = jnp.concatenate([thr_ref[...]] * rep, axis=1)
    group = A_HEADS // A_KV_HEADS
    m_ref[...] = jnp.full(m_ref.shape, NEG_INF, F32)
    l_ref[...] = jnp.zeros(l_ref.shape, F32)
    acc_ref[...] = jnp.zeros(acc_ref.shape, F32)

    def att_body(c, carry):
        k0 = pl.multiple_of(c * kc, kc)
        kk = key_ref[:, pl.ds(k0, kc)]
        bias = jnp.where(k0 + lane <= t_idx, jnp.where(kk >= thr_k, 0.0, MASK_BIAS), MASK_BIAS)
        bias = jnp.concatenate([bias] * group, axis=0)
        for c2 in range(A_KV_HEADS):
            q = qa_ref[0, c2 * group:(c2 + 1) * group].reshape(group * tq, HEAD_DIM)
            s = _nt_dot(q, ka_ref[0, c2, pl.ds(k0, kc), :]) + bias
            m_prev = m_ref[c2]
            m_next = jnp.maximum(m_prev, jnp.max(s, axis=1, keepdims=True))
            p = jnp.exp2(s - jnp.concatenate([m_next] * rep, axis=1))
            alpha = jnp.exp2(m_prev - m_next)
            l_ref[c2] = alpha * l_ref[c2] + jnp.sum(p, axis=1, keepdims=True)
            m_ref[c2] = m_next
            acc_ref[c2] = alpha * acc_ref[c2] + jnp.dot(
                p.astype(MXU_DTYPE), va_ref[0, c2, pl.ds(k0, kc), :], preferred_element_type=F32)
        return carry

    lax.fori_loop(0, nch, att_body, 0)
    for c2 in range(A_KV_HEADS):
        for g in range(group):
            rows = slice(g * tq, (g + 1) * tq)
            sl = slice((c2 * group + g) * HEAD_DIM, (c2 * group + g + 1) * HEAD_DIM)
            out = acc_ref[c2, rows, :] / l_ref[c2, rows, :]
            o_ref[:, sl] = (out * _silu(za_ref[:, sl])).astype(o_ref.dtype)


def _dsa(qi, ki, wi, qa, ka, va, proj, batch, seq, top_k, tq, kc):
    nqt = seq // tq
    n = batch * seq
    row = lambda b, i: b * nqt + i
    group = A_HEADS // A_KV_HEADS
    kern = functools.partial(_dsa_kernel, top_k=top_k, tq=tq, kc=kc)
    stat = pltpu.VMEM((A_KV_HEADS, group * tq, HEAD_DIM), F32)
    return pl.pallas_call(
        kern,
        grid=(batch, nqt),
        in_specs=[
            pl.BlockSpec((1, IDX_HEADS, tq, HEAD_DIM), lambda b, i: (b, 0, i, 0)),
            pl.BlockSpec((1, seq, HEAD_DIM), lambda b, i: (b, 0, 0)),
            pl.BlockSpec((tq, LANES), lambda b, i: (row(b, i), 0)),
            pl.BlockSpec((1, A_HEADS, tq, HEAD_DIM), lambda b, i: (b, 0, i, 0)),
            pl.BlockSpec((1, A_KV_HEADS, seq, HEAD_DIM), lambda b, i: (b, 0, 0, 0)),
            pl.BlockSpec((1, A_KV_HEADS, seq, HEAD_DIM), lambda b, i: (b, 0, 0, 0)),
            pl.BlockSpec((tq, A_WIDTH), lambda b, i: (row(b, i), COL_ZA // A_WIDTH)),
        ],
        out_specs=pl.BlockSpec((tq, A_WIDTH), lambda b, i: (row(b, i), 0)),
        out_shape=jax.ShapeDtypeStruct((n, A_WIDTH), MXU_DTYPE),
        scratch_shapes=[
            pltpu.VMEM((tq, seq), jnp.int32),
            pltpu.VMEM((IDX_HEADS, tq, kc), F32),
            pltpu.VMEM((tq, LANES), jnp.int32),
            stat, stat, stat,
        ],
        compiler_params=pltpu.CompilerParams(
            dimension_semantics=("arbitrary", "arbitrary"), vmem_limit_bytes=VMEM_LIMIT),
        name="dsa",
    )(qi, ki, wi, qa, ka, va, proj)


SB_SUB = 128


def _sb_chunk(q, kch, vch, tri2, rests, row0, key0, kc):
    nsub = q.shape[0] // SB_SUB
    z = _nt_dot(q, kch)
    his, los, lbs, sums, masks = [], [], [], [], []
    for s in range(nsub):
        zs = z[s * SB_SUB:(s + 1) * SB_SUB]
        m0 = jnp.minimum(zs, 0.0)
        t1 = m0 - zs
        sp = jnp.log2(1.0 + jnp.exp2(m0 + t1))
        log_beta = m0 - sp
        log_keep = t1 - sp
        mask = None
        first_row = row0 + s * SB_SUB
        if key0 is not None and key0 + kc - 1 >= first_row:
            mask = (key0 + lax.broadcasted_iota(jnp.int32, (SB_SUB, kc), 1)
                    < first_row + lax.broadcasted_iota(jnp.int32, (SB_SUB, kc), 0))
            log_keep = jnp.where(mask, log_keep, 0.0)
        hi = log_keep.astype(MXU_DTYPE)
        his.append(hi)
        los.append((log_keep - hi.astype(F32)).astype(MXU_DTYPE))
        lbs.append(log_beta)
        masks.append(mask)
        sums.append(jnp.sum(log_keep, axis=1, keepdims=True))
    hl = jnp.concatenate([jnp.concatenate(his, axis=0), jnp.concatenate(los, axis=0)], axis=1)
    after = jnp.dot(hl, tri2, preferred_element_type=F32)
    es = []
    for s in range(nsub):
        e = jnp.exp2(lbs[s] + after[s * SB_SUB:(s + 1) * SB_SUB] + rests[s])
        if masks[s] is not None:
            e = jnp.where(masks[s], e, 0.0)
        es.append(e.astype(MXU_DTYPE))
    pv = jnp.dot(jnp.concatenate(es, axis=0), vch, preferred_element_type=F32)
    return [r + sm for r, sm in zip(rests, sums)], pv


def _sb_kernel(q_ref, k_ref, v_ref, z_ref, o_ref, *, tq, kc, unroll):
    i = pl.program_id(2)
    nsub = tq // SB_SUB
    ndiag = tq // kc
    tri = jnp.where(lax.broadcasted_iota(jnp.int32, (kc, kc), 0) > lax.broadcasted_iota(jnp.int32, (kc, kc), 1),
                    1.0, 0.0).astype(MXU_DTYPE)
    tri2 = jnp.concatenate([tri, tri], axis=0)
    q = q_ref[0, 0]

    def kv(k0):
        k0 = pl.multiple_of(k0, kc)
        return k_ref[0, 0, pl.ds(k0, kc), :], v_ref[0, 0, pl.ds(k0, kc), :]

    rests = [jnp.zeros((SB_SUB, 1), F32) for _ in range(nsub)]
    acc = jnp.zeros((tq, HEAD_DIM), F32)
    for d in reversed(range(ndiag)):
        row0 = d * kc
        s0 = row0 // SB_SUB
        kch, vch = kv(i * tq + d * kc)
        new, pv = _sb_chunk(q[row0:], kch, vch, tri2, rests[s0:], row0, d * kc, kc)
        rests = rests[:s0] + new
        acc = jnp.concatenate([acc[:row0], acc[row0:] + pv], axis=0) if row0 else acc + pv

    nfull = i * ndiag

    def body(j, carry):
        rests, acc = list(carry[0]), carry[1]
        for u in range(unroll):
            kch, vch = kv((nfull - 1 - (j * unroll + u)) * kc)
            rests, pv = _sb_chunk(q, kch, vch, tri2, rests, 0, None, kc)
            acc = acc + pv
        return tuple(rests), acc

    _, acc = lax.fori_loop(0, nfull // unroll, body, (tuple(rests), acc))
    o_ref[...] = (acc * _silu(z_ref[...])).astype(o_ref.dtype)


def _sb(qc, kc_, vc, proj, batch, seq, tq, kc):
    nqb = seq // tq
    n = batch * seq
    col_zc = COL_ZC // HEAD_DIM
    ndiag = tq // kc
    kern = functools.partial(_sb_kernel, tq=tq, kc=kc, unroll=2 if ndiag % 2 == 0 else 1)
    return pl.pallas_call(
        kern,
        grid=(batch, C_HEADS, nqb),
        in_specs=[
            pl.BlockSpec((1, 1, tq, HEAD_DIM), lambda b, h, i: (b, h, i, 0)),
            pl.BlockSpec((1, 1, seq, HEAD_DIM), lambda b, h, i: (b, h, 0, 0)),
            pl.BlockSpec((1, 1, seq, HEAD_DIM), lambda b, h, i: (b, h, 0, 0)),
            pl.BlockSpec((tq, HEAD_DIM), lambda b, h, i: (b * nqb + i, col_zc + h)),
        ],
        out_specs=pl.BlockSpec((tq, HEAD_DIM), lambda b, h, i: (b * nqb + i, h)),
        out_shape=jax.ShapeDtypeStruct((n, C_WIDTH), MXU_DTYPE),
        compiler_params=pltpu.CompilerParams(
            dimension_semantics=("arbitrary", "arbitrary", "arbitrary"), vmem_limit_bytes=VMEM_LIMIT),
        name="sb",
    )(qc, kc_, vc, proj)


CONV_HIST = 32


def _conv_kernel(u_ref, zb_ref, cw_ref, cb_ref, lw_ref, lb_ref, o_ref, ysh, *, ts, width, rc):
    ch = o_ref.shape[1]

    @pl.when(pl.program_id(1) == 0)
    def _():
        ysh[0, 0:CONV_HIST, :] = jnp.zeros((CONV_HIST, ch), F32)

    @pl.when(pl.program_id(1) > 0)
    def _():
        ysh[0, 0:CONV_HIST, :] = ysh[0, ts:ts + CONV_HIST, :]

    ysh[0, CONV_HIST:CONV_HIST + ts, :] = u_ref[:, 0:ch] * jax.nn.sigmoid(u_ref[:, ch:2 * ch])
    span = CONV_HIST + ts - SUBLANES
    for r in range(1, SUBLANES):
        ysh[r, 0:span, :] = ysh[0, r:r + span, :]

    def chunk(c, carry):
        r0 = pl.multiple_of(c * rc, rc)
        acc = jnp.broadcast_to(cb_ref[...], (rc, ch))
        for j in range(width):
            off = CONV_HIST - (width - 1) + j
            r = off % SUBLANES
            acc = acc + cw_ref[j:j + 1, :] * ysh[r, pl.ds(r0 + (off - r), rc), :]
        mu = jnp.mean(acc, axis=1, keepdims=True)
        d = acc - mu
        var = jnp.mean(d * d, axis=1, keepdims=True)
        y = d * lax.rsqrt(var + EPS) * lw_ref[...] + lb_ref[...]
        o_ref[pl.ds(r0, rc), :] = (_silu(y) * _silu(zb_ref[pl.ds(r0, rc), :])).astype(o_ref.dtype)
        return carry

    lax.fori_loop(0, ts // rc, chunk, 0)


def _conv(proj, conv_w, conv_b, ln_w, ln_b, batch, seq, ts, rc):
    n = batch * seq
    width, ch = conv_w.shape
    nst = seq // ts
    col_ub = COL_UB // (2 * ch)
    vec = pl.BlockSpec((1, ch), lambda b, s: (0, 0))
    kern = functools.partial(_conv_kernel, ts=ts, width=width, rc=rc)
    return pl.pallas_call(
        kern,
        grid=(batch, nst),
        in_specs=[
            pl.BlockSpec((ts, 2 * ch), lambda b, s: (b * nst + s, col_ub)),
            pl.BlockSpec((ts, ch), lambda b, s: (b * nst + s, COL_ZB // ch)),
            pl.BlockSpec((width, ch), lambda b, s: (0, 0)),
            vec, vec, vec,
        ],
        out_specs=pl.BlockSpec((ts, ch), lambda b, s: (b * nst + s, 0)),
        out_shape=jax.ShapeDtypeStruct((n, ch), MXU_DTYPE),
        scratch_shapes=[pltpu.VMEM((SUBLANES, CONV_HIST + ts, ch), F32)],
        compiler_params=pltpu.CompilerParams(
            dimension_semantics=("arbitrary", "arbitrary"), vmem_limit_bytes=VMEM_LIMIT),
        name="conv",
    )(proj, proj, conv_w, conv_b.reshape(1, ch), ln_w.reshape(1, ch), ln_b.reshape(1, ch))


def _merge_kernel(ya_ref, yb_ref, yc_ref, g0_ref, g1_ref, g2_ref, wa_ref, wb_ref, wc_ref, o_ref):
    ya = jnp.dot(ya_ref[...], wa_ref[...], preferred_element_type=F32)
    yb = jnp.dot(yb_ref[...], wb_ref[...], preferred_element_type=F32)
    yc = jnp.dot(yc_ref[...], wc_ref[...], preferred_element_type=F32)
    merged = (jax.nn.sigmoid(g0_ref[...]) * ya + jax.nn.sigmoid(g1_ref[...]) * yb
              + jax.nn.sigmoid(g2_ref[...]) * yc)
    o_ref[...] = merged.astype(o_ref.dtype)


def _merge(ya, yb, yc, proj, wa, wb, wc, col_gates, tm):
    n = ya.shape[0]
    d = wa.shape[1]
    act = lambda w: pl.BlockSpec((tm, w), lambda i: (i, 0))
    gate = lambda k: pl.BlockSpec((tm, d), lambda i, k=k: (i, col_gates // d + k))
    wspec = lambda w: pl.BlockSpec(w.shape, lambda i: (0, 0))
    return pl.pallas_call(
        _merge_kernel,
        grid=(n // tm,),
        in_specs=[act(ya.shape[1]), act(yb.shape[1]), act(yc.shape[1]), gate(0), gate(1), gate(2),
                  wspec(wa), wspec(wb), wspec(wc)],
        out_specs=pl.BlockSpec((tm, d), lambda i: (i, 0)),
        out_shape=jax.ShapeDtypeStruct((n, d), MXU_DTYPE),
        compiler_params=pltpu.CompilerParams(
            dimension_semantics=("arbitrary",), vmem_limit_bytes=VMEM_LIMIT),
        name="merge",
    )(ya, yb, yc, proj, proj, proj, wa, wb, wc)


def _outproj_kernel(m_ref, x_ref, w_ref, fw_ref, o_ref, *, final_norm):
    y = x_ref[...] + jnp.dot(m_ref[...], w_ref[...], preferred_element_type=F32)
    if final_norm:
        ms = jnp.mean(y * y, axis=-1, keepdims=True)
        y = y * lax.rsqrt(ms + EPS) * fw_ref[...]
    o_ref[...] = y


def _outproj(merged, x2d, w_out, final_w, final_norm, tm):
    n, d = x2d.shape
    kern = functools.partial(_outproj_kernel, final_norm=final_norm)
    return pl.pallas_call(
        kern,
        grid=(n // tm,),
        in_specs=[
            pl.BlockSpec((tm, d), lambda i: (i, 0)),
            pl.BlockSpec((tm, d), lambda i: (i, 0)),
            pl.BlockSpec((d, d), lambda i: (0, 0)),
            pl.BlockSpec((1, d), lambda i: (0, 0)),
        ],
        out_specs=pl.BlockSpec((tm, d), lambda i: (i, 0)),
        out_shape=jax.ShapeDtypeStruct((n, d), F32),
        compiler_params=pltpu.CompilerParams(
            dimension_semantics=("arbitrary",), vmem_limit_bytes=VMEM_LIMIT),
        name="outproj",
    )(merged, x2d, w_out, final_w.reshape(1, d))


def _prepare_w_in(w_in, d, conv_ch):
    sizes = (A_WIDTH, A_KV_WIDTH, A_KV_WIDTH, A_WIDTH, IDX_WIDTH, HEAD_DIM, IDX_HEADS,
             2 * conv_ch, conv_ch, C_WIDTH, C_WIDTH, C_WIDTH, C_WIDTH, N_BRANCH * d)
    offs = [0]
    for s in sizes:
        offs.append(offs[-1] + s)
    assert offs[-1] == w_in.shape[-1]
    piece = lambda k: w_in[..., offs[k]:offs[k + 1]]
    qa, ka, va, za, qi, ki, wi, ub, zb, qc, kc, vc, zc, gates = (piece(k) for k in range(len(sizes)))
    zeros = lambda w: jnp.zeros(w_in.shape[:-1] + (w,), w_in.dtype)
    misc_used = 2 * A_KV_WIDTH + HEAD_DIM + LANES
    cols = [qi, qa, qc, kc, vc,
            ka, va, ki, wi, zeros(LANES - IDX_HEADS), zeros(COL_TILE - misc_used),
            za, zb, zc, ub, gates]
    used = COL_GATES + N_BRANCH * d
    np_ = -(-used // COL_TILE) * COL_TILE
    if np_ > used:
        cols.append(zeros(np_ - used))
    return jnp.concatenate(cols, axis=-1).astype(MXU_DTYPE)


def _rope_tables(seq):
    inv = 1.0 / (ROPE_THETA ** (jnp.arange(0, HEAD_DIM, 2, dtype=F32) / HEAD_DIM))
    ang = jnp.arange(seq, dtype=F32)[:, None] * inv[None, :]
    cos, sin = jnp.cos(ang), jnp.sin(ang)
    return jnp.concatenate([cos, cos], axis=1), jnp.concatenate([-sin, sin], axis=1)


def kernel(x, norm_w, w_in, conv_w, conv_b, conv_ln_w, conv_ln_b, w_a_out, w_b_out, w_c_out, w_out, final_norm_w):
    batch, seq, d = x.shape
    depth = norm_w.shape[0]
    conv_ch = conv_b.shape[-1]
    n = batch * seq
    assert seq % BLOCK_Q == 0 and conv_ch == COL_TILE and conv_w.shape[1] - 1 <= CONV_HIST
    top_k = min(IDX_TOPK, seq // 4)
    kc = 256
    assert seq % kc == 0 and COL_GATES % d == 0

    w_prep = _prepare_w_in(w_in, d, conv_ch)
    wa, wb, wc, wo = (w.astype(MXU_DTYPE) for w in (w_a_out, w_b_out, w_c_out, w_out))
    cw = conv_w.reshape(depth, conv_w.shape[1], conv_ch)
    cos_t, sin_t = _rope_tables(seq)

    tm_in = min(1024, n)
    ts_prep = min(256, seq)
    ts_conv = min(512, seq)
    tq_dsa = min(256, seq)
    tq_sb = min(512, seq)
    assert tq_dsa % kc == 0 and tq_sb % kc == 0 and seq % tq_sb == 0

    x2d = x.reshape(n, d)
    for layer in range(depth):
        proj = _inproj(x2d, norm_w[layer], w_prep[layer], tm_in, COL_TILE)
        qi, qa, qc, kcs, vcs, ka, va, ki, wi = _prep(proj, cos_t, sin_t, batch, seq, ts_prep)
        ya = _dsa(qi, ki, wi, qa, ka, va, proj, batch, seq, top_k, tq_dsa, kc)
        yc = _sb(qc, kcs, vcs, proj, batch, seq, tq_sb, kc)
        yb = _conv(proj, cw[layer], conv_b[layer], conv_ln_w[layer], conv_ln_b[layer], batch, seq, ts_conv, 32)
        merged = _merge(ya, yb, yc, proj, wa[layer], wb[layer], wc[layer], COL_GATES, min(256, n))
        x2d = _outproj(merged, x2d, wo[layer], final_norm_w, layer == depth - 1, min(512, n))
    return x2d.reshape(batch, seq, d)
```

```python
import functools

import jax
import jax.numpy as jnp
from jax import lax
from jax.experimental import pallas as pl
from jax.experimental.pallas import tpu as pltpu

HEAD_DIM = 128
ROPE_THETA = 10000.0
BLOCK_Q = 128
EPS = 1e-6
NEG_INF = -1e30
MASK_BIAS = -2e30
A_HEADS = 8
A_KV_HEADS = 2
IDX_HEADS = 16
IDX_TOPK = 256
C_HEADS = 8
N_BRANCH = 3
LANES = 128
SUBLANES = 8
LOG2E = 1.4426950408889634
INT_MIN = -(2 ** 31)

MXU_DTYPE = jnp.bfloat16
F32 = jnp.float32

A_WIDTH = A_HEADS * HEAD_DIM
A_KV_WIDTH = A_KV_HEADS * HEAD_DIM
IDX_WIDTH = IDX_HEADS * HEAD_DIM
C_WIDTH = C_HEADS * HEAD_DIM

COL_TILE = 1024
CONV_CH = COL_TILE
COL_QA = 0
COL_KA = COL_QA + A_WIDTH
COL_VA = COL_KA + A_KV_WIDTH
COL_ZA = COL_VA + A_KV_WIDTH
COL_QI = COL_ZA + A_WIDTH
COL_KI = COL_QI + IDX_WIDTH
COL_WI = COL_KI + HEAD_DIM
FIRST_RUN = COL_WI + IDX_HEADS
COL_UB = -(-FIRST_RUN // COL_TILE) * COL_TILE
COL_ZB = COL_UB + 2 * CONV_CH
COL_QC = COL_ZB + CONV_CH
COL_KC = COL_QC + C_WIDTH
COL_VC = COL_KC + C_WIDTH
COL_ZC = COL_VC + C_WIDTH
COL_GATES = COL_ZC + C_WIDTH
HALF_TILE = COL_TILE // 2

VMEM_LIMIT = 56 * 1024 * 1024


def _nt_dot(a, b):
    return lax.dot_general(a, b, (((1,), (1,)), ((), ())), preferred_element_type=F32)


def _silu(x):
    return x * jax.nn.sigmoid(x)


def _inproj_kernel(x_ref, nw_ref, w_ref, o_ref, h_ref):
    @pl.when(pl.program_id(1) == 0)
    def _():
        xf = x_ref[...]
        ms = jnp.mean(xf * xf, axis=-1, keepdims=True)
        h_ref[...] = (xf * lax.rsqrt(ms + EPS) * nw_ref[...]).astype(h_ref.dtype)

    o_ref[...] = jnp.dot(h_ref[...], w_ref[...], preferred_element_type=F32)


def _inproj(x2d, norm_w, w_prep, layer, tm, tn):
    n, d = x2d.shape
    np_ = w_prep.shape[2]
    return pl.pallas_call(
        _inproj_kernel,
        grid=(n // tm, np_ // tn),
        in_specs=[
            pl.BlockSpec((tm, d), lambda i, j: (i, 0)),
            pl.BlockSpec((1, d), lambda i, j: (0, 0)),
            pl.BlockSpec((None, d, tn), lambda i, j: (layer, 0, j)),
        ],
        out_specs=pl.BlockSpec((tm, tn), lambda i, j: (i, j)),
        out_shape=jax.ShapeDtypeStruct((n, np_), F32),
        scratch_shapes=[pltpu.VMEM((tm, d), MXU_DTYPE)],
        compiler_params=pltpu.CompilerParams(
            dimension_semantics=("arbitrary", "arbitrary"), vmem_limit_bytes=VMEM_LIMIT),
        name="inproj",
    )(x2d, norm_w.reshape(1, d), w_prep)


def _rope(x, cos, sin_signed):
    return x * cos + pltpu.roll(x, HEAD_DIM // 2, axis=1) * sin_signed


def _prep_kernel(qa_ref, kva_ref, qi0_ref, qi1_ref, qi2_ref, qi3_ref, ki_ref, wi_ref, qc_ref, kc_ref, vc_ref,
                 cos_ref, sin_ref,
                 qi_o, qa_o, qc_o, kc_o, vc_o, ka_o, va_o, ki_o, wi_o, *, qk_scale, wi_scale):
    cos = cos_ref[...]
    sin = sin_ref[...]
    heads_per_half = HALF_TILE // HEAD_DIM
    for t, src in enumerate((qi0_ref, qi1_ref, qi2_ref, qi3_ref)):
        for h in range(heads_per_half):
            x = src[:, h * HEAD_DIM:(h + 1) * HEAD_DIM]
            qi_o[0, t * heads_per_half + h] = _rope(x, cos, sin).astype(qi_o.dtype)
    for h in range(A_HEADS):
        x = qa_ref[:, h * HEAD_DIM:(h + 1) * HEAD_DIM]
        qa_o[0, h] = (_rope(x, cos, sin) * qk_scale).astype(qa_o.dtype)
    for h in range(C_HEADS):
        sl = slice(h * HEAD_DIM, (h + 1) * HEAD_DIM)
        qc_o[0, h] = (qc_ref[:, sl] * qk_scale).astype(qc_o.dtype)
        kc_o[0, h] = kc_ref[:, sl].astype(kc_o.dtype)
        vc_o[0, h] = vc_ref[:, sl].astype(vc_o.dtype)
    for h in range(A_KV_HEADS):
        ka_o[0, h] = _rope(kva_ref[:, h * HEAD_DIM:(h + 1) * HEAD_DIM], cos, sin).astype(ka_o.dtype)
        va_o[0, h] = kva_ref[:, A_KV_WIDTH + h * HEAD_DIM:A_KV_WIDTH + (h + 1) * HEAD_DIM].astype(va_o.dtype)
    ki_o[0] = _rope(ki_ref[...], cos, sin).astype(ki_o.dtype)
    wi_o[...] = wi_ref[...] * wi_scale[0] * wi_scale[1]


def _prep(proj, cos_t, sin_t, batch, seq, ts):
    n = proj.shape[0]
    nst = seq // ts
    row = lambda b, s: b * nst + s
    col = lambda off, w: pl.BlockSpec((ts, w), lambda b, s: (row(b, s), off // w))
    tab = pl.BlockSpec((ts, HEAD_DIM), lambda b, s: (s, 0))
    in_cols = ([col(COL_QA, A_WIDTH), col(COL_KA, 2 * A_KV_WIDTH)]
               + [col(COL_QI + t * HALF_TILE, HALF_TILE) for t in range(IDX_WIDTH // HALF_TILE)]
               + [col(COL_KI, HEAD_DIM), col(COL_WI, LANES),
                  col(COL_QC, C_WIDTH), col(COL_KC, C_WIDTH), col(COL_VC, C_WIDTH)])
    heads = lambda nh: pl.BlockSpec((1, nh, ts, HEAD_DIM), lambda b, s: (b, 0, s, 0))
    hshape = lambda nh: jax.ShapeDtypeStruct((batch, nh, seq, HEAD_DIM), MXU_DTYPE)
    kern = functools.partial(
        _prep_kernel, qk_scale=HEAD_DIM ** -0.5 * LOG2E, wi_scale=(IDX_HEADS ** -0.5, HEAD_DIM ** -0.5))
    return pl.pallas_call(
        kern,
        grid=(batch, nst),
        in_specs=in_cols + [tab, tab],
        out_specs=[
            heads(IDX_HEADS), heads(A_HEADS), heads(C_HEADS), heads(C_HEADS), heads(C_HEADS),
            heads(A_KV_HEADS), heads(A_KV_HEADS),
            pl.BlockSpec((1, ts, HEAD_DIM), lambda b, s: (b, s, 0)),
            pl.BlockSpec((ts, LANES), lambda b, s: (row(b, s), 0)),
        ],
        out_shape=[
            hshape(IDX_HEADS), hshape(A_HEADS), hshape(C_HEADS), hshape(C_HEADS), hshape(C_HEADS),
            hshape(A_KV_HEADS), hshape(A_KV_HEADS),
            jax.ShapeDtypeStruct((batch, seq, HEAD_DIM), MXU_DTYPE),
            jax.ShapeDtypeStruct((n, LANES), F32),
        ],
        compiler_params=pltpu.CompilerParams(
            dimension_semantics=("arbitrary", "arbitrary"), vmem_limit_bytes=VMEM_LIMIT),
        name="prep",
    )(*([proj] * len(in_cols)), cos_t, sin_t)


def _sortable_key(x):
    bits = lax.bitcast_convert_type(x, jnp.int32)
    return jnp.where(bits < 0, bits ^ jnp.int32(0x7FFFFFFF), bits)


IDX_HEAD_GROUP = 4


PACK_ROWS = 16
HALF_OFFSET = 32768


def _dsa_kernel(qi_ref, ki_ref, wi_ref, qa_ref, ka_ref, va_ref, za0_ref, za1_ref, o_ref,
                key_ref, hi_ref, lo_ref, wb_ref, thr_ref, m_ref, l_ref, acc_ref, *, top_k, tq, kc):
    i = pl.program_id(1)
    nch = ((i + 1) * tq) // kc
    t_idx = i * tq + lax.broadcasted_iota(jnp.int32, (tq, kc), 0)
    lane = lax.broadcasted_iota(jnp.int32, (tq, kc), 1)
    rep = kc // LANES

    wi = wi_ref[...]
    for h in range(IDX_HEADS):
        wb_ref[h] = jnp.broadcast_to(wi[:, h:h + 1], (tq, kc))

    def score_body(c, carry):
        k0 = pl.multiple_of(c * kc, kc)
        kch = ki_ref[0, pl.ds(k0, kc), :]
        acc = None
        for g in range(IDX_HEADS // IDX_HEAD_GROUP):
            qg = qi_ref[0, g * IDX_HEAD_GROUP:(g + 1) * IDX_HEAD_GROUP].reshape(IDX_HEAD_GROUP * tq, HEAD_DIM)
            rel = _nt_dot(qg, kch)
            for hh in range(IDX_HEAD_GROUP):
                term = wb_ref[g * IDX_HEAD_GROUP + hh] * jnp.maximum(rel[hh * tq:(hh + 1) * tq], 0.0)
                acc = term if acc is None else acc + term
        score = jnp.where(k0 + lane <= t_idx, acc, NEG_INF)
        key = _sortable_key(score)
        key_ref[:, pl.ds(k0, kc)] = key
        key_t = key.T
        hi_ref[pl.ds(k0, kc), :] = lax.shift_right_arithmetic(key_t, 16).astype(jnp.int16)
        lo_ref[pl.ds(k0, kc), :] = ((key_t & 0xFFFF) - HALF_OFFSET).astype(jnp.int16)
        return carry

    lax.fori_loop(0, nch, score_body, 0)

    need_bisect = (i + 1) * tq > top_k

    @pl.when(jnp.logical_not(need_bisect))
    def _():
        thr_ref[...] = jnp.full(thr_ref.shape, INT_MIN, jnp.int32)

    @pl.when(need_bisect)
    def _():
        zero = jnp.zeros((PACK_ROWS, tq), jnp.int16)

        def count(ref, pred):
            def body(c, cnt):
                v = ref[pl.ds(pl.multiple_of(c * kc, kc), kc), :]
                ind = jnp.where(pred(v), jnp.int16(1), jnp.int16(0))
                for r in range(kc // PACK_ROWS):
                    cnt = cnt + ind[r * PACK_ROWS:(r + 1) * PACK_ROWS]
                return cnt
            cnt = lax.fori_loop(0, nch, body, zero)
            return jnp.sum(cnt.astype(F32), axis=0, keepdims=True)

        def bisect16(ref, need):
            def bit_body(b, u):
                ub = u | lax.shift_left(jnp.int32(1), 15 - b)
                cand = jnp.broadcast_to((ub - HALF_OFFSET).astype(jnp.int16), (kc, tq))
                tot = count(ref, lambda v: v >= cand)
                return jnp.where(tot >= need, ub, u)
            return lax.fori_loop(0, 16, bit_body, jnp.zeros((1, tq), jnp.int32))

        u_hi = bisect16(hi_ref, float(top_k))
        t_hi = jnp.broadcast_to((u_hi - HALF_OFFSET).astype(jnp.int16), (kc, tq))
        above = count(hi_ref, lambda v: v > t_hi)

        def mask_body(c, carry):
            sl = pl.ds(pl.multiple_of(c * kc, kc), kc)
            lo_ref[sl, :] = jnp.where(hi_ref[sl, :] == t_hi, lo_ref[sl, :], jnp.int16(-HALF_OFFSET))
            return carry

        lax.fori_loop(0, nch, mask_body, 0)
        u_lo = bisect16(lo_ref, float(top_k) - above)
        thr_q = (u_hi - HALF_OFFSET) * (2 * HALF_OFFSET) + u_lo
        t_q = i * tq + lax.broadcasted_iota(jnp.int32, (1, tq), 1)
        thr_q = jnp.where(t_q < top_k, jnp.int32(INT_MIN), thr_q)
        for r in range(tq // LANES):
            blk = jnp.broadcast_to(thr_q[:, r * LANES:(r + 1) * LANES], (LANES, LANES))
            thr_ref[r * LANES:(r + 1) * LANES, :] = blk.T

    thr_k = jnp.concatenate([thr_ref[...]] * rep, axis=1)
    group = A_HEADS // A_KV_HEADS
    m_ref[...] = jnp.full(m_ref.shape, NEG_INF, F32)
    l_ref[...] = jnp.zeros(l_ref.shape, F32)
    acc_ref[...] = jnp.zeros(acc_ref.shape, F32)

    def att_body(c, carry):
        k0 = pl.multiple_of(c * kc, kc)
        kk = key_ref[:, pl.ds(k0, kc)]
        bias = jnp.where(k0 + lane <= t_idx, jnp.where(kk >= thr_k, 0.0, MASK_BIAS), MASK_BIAS)
        bias = jnp.concatenate([bias] * group, axis=0)
        for c2 in range(A_KV_HEADS):
            q = qa_ref[0, c2 * group:(c2 + 1) * group].reshape(group * tq, HEAD_DIM)
            s = _nt_dot(q, ka_ref[0, c2, pl.ds(k0, kc), :]) + bias
            m_prev = m_ref[c2]
            m_next = jnp.maximum(m_prev, jnp.max(s, axis=1, keepdims=True))
            p = jnp.exp2(s - jnp.concatenate([m_next] * rep, axis=1))
            alpha = jnp.exp2(m_prev - m_next)
            l_ref[c2] = alpha * l_ref[c2] + jnp.sum(p, axis=1, keepdims=True)
            m_ref[c2] = m_next
            acc_ref[c2] = alpha * acc_ref[c2] + jnp.dot(
                p.astype(MXU_DTYPE), va_ref[0, c2, pl.ds(k0, kc), :], preferred_element_type=F32)
        return carry

    lax.fori_loop(0, nch, att_body, 0)
    heads_per_half = HALF_TILE // HEAD_DIM
    for c2 in range(A_KV_HEADS):
        for g in range(group):
            h = c2 * group + g
            rows = slice(g * tq, (g + 1) * tq)
            out = acc_ref[c2, rows, :] / l_ref[c2, rows, :]
            z_ref = (za0_ref, za1_ref)[h // heads_per_half]
            za = z_ref[:, (h % heads_per_half) * HEAD_DIM:(h % heads_per_half + 1) * HEAD_DIM]
            o_ref[:, h * HEAD_DIM:(h + 1) * HEAD_DIM] = (out * _silu(za)).astype(o_ref.dtype)


def _dsa(qi, ki, wi, qa, ka, va, proj, batch, seq, top_k, tq, kc):
    nqt = seq // tq
    n = batch * seq
    row = lambda b, i: b * nqt + i
    group = A_HEADS // A_KV_HEADS
    kern = functools.partial(_dsa_kernel, top_k=top_k, tq=tq, kc=kc)
    stat = pltpu.VMEM((A_KV_HEADS, group * tq, HEAD_DIM), F32)
    return pl.pallas_call(
        kern,
        grid=(batch, nqt),
        in_specs=[
            pl.BlockSpec((1, IDX_HEADS, tq, HEAD_DIM), lambda b, i: (b, 0, i, 0)),
            pl.BlockSpec((1, seq, HEAD_DIM), lambda b, i: (b, 0, 0)),
            pl.BlockSpec((tq, LANES), lambda b, i: (row(b, i), 0)),
            pl.BlockSpec((1, A_HEADS, tq, HEAD_DIM), lambda b, i: (b, 0, i, 0)),
            pl.BlockSpec((1, A_KV_HEADS, seq, HEAD_DIM), lambda b, i: (b, 0, 0, 0)),
            pl.BlockSpec((1, A_KV_HEADS, seq, HEAD_DIM), lambda b, i: (b, 0, 0, 0)),
            pl.BlockSpec((tq, HALF_TILE), lambda b, i: (row(b, i), COL_ZA // HALF_TILE)),
            pl.BlockSpec((tq, HALF_TILE), lambda b, i: (row(b, i), COL_ZA // HALF_TILE + 1)),
        ],
        out_specs=pl.BlockSpec((tq, A_WIDTH), lambda b, i: (row(b, i), 0)),
        out_shape=jax.ShapeDtypeStruct((n, A_WIDTH), MXU_DTYPE),
        scratch_shapes=[
            pltpu.VMEM((tq, seq), jnp.int32),
            pltpu.VMEM((seq, tq), jnp.int16),
            pltpu.VMEM((seq, tq), jnp.int16),
            pltpu.VMEM((IDX_HEADS, tq, kc), F32),
            pltpu.VMEM((tq, LANES), jnp.int32),
            stat, stat, stat,
        ],
        compiler_params=pltpu.CompilerParams(
            dimension_semantics=("arbitrary", "arbitrary"), vmem_limit_bytes=VMEM_LIMIT),
        name="dsa",
    )(qi, ki, wi, qa, ka, va, proj, proj)


SB_SUB = 128


def _sb_chunk(q, kch, vch, tri2, rests, row0, key0, kc):
    nsub = q.shape[0] // SB_SUB
    z = _nt_dot(q, kch)
    his, los, lbs, sums, masks = [], [], [], [], []
    for s in range(nsub):
        zs = z[s * SB_SUB:(s + 1) * SB_SUB]
        m0 = jnp.minimum(zs, 0.0)
        t1 = m0 - zs
        sp = jnp.log2(1.0 + jnp.exp2(m0 + t1))
        log_beta = m0 - sp
        log_keep = t1 - sp
        mask = None
        first_row = row0 + s * SB_SUB
        if key0 is not None and key0 + kc - 1 >= first_row:
            mask = (key0 + lax.broadcasted_iota(jnp.int32, (SB_SUB, kc), 1)
                    < first_row + lax.broadcasted_iota(jnp.int32, (SB_SUB, kc), 0))
            log_keep = jnp.where(mask, log_keep, 0.0)
        hi = log_keep.astype(MXU_DTYPE)
        his.append(hi)
        los.append((log_keep - hi.astype(F32)).astype(MXU_DTYPE))
        lbs.append(log_beta)
        masks.append(mask)
        sums.append(jnp.sum(log_keep, axis=1, keepdims=True))
    hl = jnp.concatenate([jnp.concatenate(his, axis=0), jnp.concatenate(los, axis=0)], axis=1)
    after = jnp.dot(hl, tri2, preferred_element_type=F32)
    es = []
    for s in range(nsub):
        e = jnp.exp2(lbs[s] + after[s * SB_SUB:(s + 1) * SB_SUB] + rests[s])
        if masks[s] is not None:
            e = jnp.where(masks[s], e, 0.0)
        es.append(e.astype(MXU_DTYPE))
    pv = jnp.dot(jnp.concatenate(es, axis=0), vch, preferred_element_type=F32)
    return [r + sm for r, sm in zip(rests, sums)], pv


def _sb_kernel(q_ref, k_ref, v_ref, z_ref, o_ref, *, tq, kc, unroll):
    i = pl.program_id(2)
    nsub = tq // SB_SUB
    ndiag = tq // kc
    tri = jnp.where(lax.broadcasted_iota(jnp.int32, (kc, kc), 0) > lax.broadcasted_iota(jnp.int32, (kc, kc), 1),
                    1.0, 0.0).astype(MXU_DTYPE)
    tri2 = jnp.concatenate([tri, tri], axis=0)
    q = q_ref[0, 0]

    def kv(k0):
        k0 = pl.multiple_of(k0, kc)
        return k_ref[0, 0, pl.ds(k0, kc), :], v_ref[0, 0, pl.ds(k0, kc), :]

    rests = [jnp.zeros((SB_SUB, 1), F32) for _ in range(nsub)]
    acc = jnp.zeros((tq, HEAD_DIM), F32)
    for d in reversed(range(ndiag)):
        row0 = d * kc
        s0 = row0 // SB_SUB
        kch, vch = kv(i * tq + d * kc)
        new, pv = _sb_chunk(q[row0:], kch, vch, tri2, rests[s0:], row0, d * kc, kc)
        rests = rests[:s0] + new
        acc = jnp.concatenate([acc[:row0], acc[row0:] + pv], axis=0) if row0 else acc + pv

    nfull = i * ndiag

    def body(j, carry):
        rests, acc = list(carry[0]), carry[1]
        for u in range(unroll):
            kch, vch = kv((nfull - 1 - (j * unroll + u)) * kc)
            rests, pv = _sb_chunk(q, kch, vch, tri2, rests, 0, None, kc)
            acc = acc + pv
        return tuple(rests), acc

    _, acc = lax.fori_loop(0, nfull // unroll, body, (tuple(rests), acc))
    o_ref[...] = (acc * _silu(z_ref[...])).astype(o_ref.dtype)


def _sb(qc, kc_, vc, proj, batch, seq, tq, kc):
    nqb = seq // tq
    n = batch * seq
    col_zc = COL_ZC // HEAD_DIM
    ndiag = tq // kc
    kern = functools.partial(_sb_kernel, tq=tq, kc=kc, unroll=2 if ndiag % 2 == 0 else 1)
    return pl.pallas_call(
        kern,
        grid=(batch, C_HEADS, nqb),
        in_specs=[
            pl.BlockSpec((1, 1, tq, HEAD_DIM), lambda b, h, i: (b, h, i, 0)),
            pl.BlockSpec((1, 1, seq, HEAD_DIM), lambda b, h, i: (b, h, 0, 0)),
            pl.BlockSpec((1, 1, seq, HEAD_DIM), lambda b, h, i: (b, h, 0, 0)),
            pl.BlockSpec((tq, HEAD_DIM), lambda b, h, i: (b * nqb + i, col_zc + h)),
        ],
        out_specs=pl.BlockSpec((tq, HEAD_DIM), lambda b, h, i: (b * nqb + i, h)),
        out_shape=jax.ShapeDtypeStruct((n, C_WIDTH), MXU_DTYPE),
        compiler_params=pltpu.CompilerParams(
            dimension_semantics=("arbitrary", "arbitrary", "arbitrary"), vmem_limit_bytes=VMEM_LIMIT),
        name="sb",
    )(qc, kc_, vc, proj)


CONV_HIST = 32


def _conv_kernel(ua_ref, ug_ref, zb_ref, cw_ref, cb_ref, lw_ref, lb_ref, o_ref, ysh, *, ts, width, rc):
    ch = o_ref.shape[1]

    @pl.when(pl.program_id(1) == 0)
    def _():
        ysh[0, 0:CONV_HIST, :] = jnp.zeros((CONV_HIST, ch), F32)

    @pl.when(pl.program_id(1) > 0)
    def _():
        ysh[0, 0:CONV_HIST, :] = ysh[0, ts:ts + CONV_HIST, :]

    ysh[0, CONV_HIST:CONV_HIST + ts, :] = ua_ref[...] * jax.nn.sigmoid(ug_ref[...])
    span = CONV_HIST + ts - SUBLANES
    for r in range(1, SUBLANES):
        ysh[r, 0:span, :] = ysh[0, r:r + span, :]

    def chunk(c, carry):
        r0 = pl.multiple_of(c * rc, rc)
        acc = jnp.broadcast_to(cb_ref[...], (rc, ch))
        for j in range(width):
            off = CONV_HIST - (width - 1) + j
            r = off % SUBLANES
            acc = acc + cw_ref[j:j + 1, :] * ysh[r, pl.ds(r0 + (off - r), rc), :]
        mu = jnp.mean(acc, axis=1, keepdims=True)
        d = acc - mu
        var = jnp.mean(d * d, axis=1, keepdims=True)
        y = d * lax.rsqrt(var + EPS) * lw_ref[...] + lb_ref[...]
        o_ref[pl.ds(r0, rc), :] = (_silu(y) * _silu(zb_ref[pl.ds(r0, rc), :])).astype(o_ref.dtype)
        return carry

    lax.fori_loop(0, ts // rc, chunk, 0)


def _conv(proj, conv_w, conv_b, ln_w, ln_b, batch, seq, ts, rc):
    n = batch * seq
    width, ch = conv_w.shape
    nst = seq // ts
    vec = pl.BlockSpec((1, ch), lambda b, s: (0, 0))
    kern = functools.partial(_conv_kernel, ts=ts, width=width, rc=rc)
    return pl.pallas_call(
        kern,
        grid=(batch, nst),
        in_specs=[
            pl.BlockSpec((ts, ch), lambda b, s: (b * nst + s, COL_UB // ch)),
            pl.BlockSpec((ts, ch), lambda b, s: (b * nst + s, COL_UB // ch + 1)),
            pl.BlockSpec((ts, ch), lambda b, s: (b * nst + s, COL_ZB // ch)),
            pl.BlockSpec((width, ch), lambda b, s: (0, 0)),
            vec, vec, vec,
        ],
        out_specs=pl.BlockSpec((ts, ch), lambda b, s: (b * nst + s, 0)),
        out_shape=jax.ShapeDtypeStruct((n, ch), MXU_DTYPE),
        scratch_shapes=[pltpu.VMEM((SUBLANES, CONV_HIST + ts, ch), F32)],
        compiler_params=pltpu.CompilerParams(
            dimension_semantics=("arbitrary", "arbitrary"), vmem_limit_bytes=VMEM_LIMIT),
        name="conv",
    )(proj, proj, proj, conv_w, conv_b.reshape(1, ch), ln_w.reshape(1, ch), ln_b.reshape(1, ch))


def _merge_kernel(ya_ref, yb_ref, yc_ref, g0_ref, g1_ref, g2_ref, wa_ref, wb_ref, wc_ref, o_ref):
    ya = jnp.dot(ya_ref[...], wa_ref[...], preferred_element_type=F32)
    yb = jnp.dot(yb_ref[...], wb_ref[...], preferred_element_type=F32)
    yc = jnp.dot(yc_ref[...], wc_ref[...], preferred_element_type=F32)
    merged = (jax.nn.sigmoid(g0_ref[...]) * ya + jax.nn.sigmoid(g1_ref[...]) * yb
              + jax.nn.sigmoid(g2_ref[...]) * yc)
    o_ref[...] = merged.astype(o_ref.dtype)


def _merge(ya, yb, yc, proj, wa, wb, wc, col_gates, tm):
    n = ya.shape[0]
    d = wa.shape[1]
    act = lambda w: pl.BlockSpec((tm, w), lambda i: (i, 0))
    gate = lambda k: pl.BlockSpec((tm, d), lambda i, k=k: (i, col_gates // d + k))
    wspec = lambda w: pl.BlockSpec(w.shape, lambda i: (0, 0))
    return pl.pallas_call(
        _merge_kernel,
        grid=(n // tm,),
        in_specs=[act(ya.shape[1]), act(yb.shape[1]), act(yc.shape[1]), gate(0), gate(1), gate(2),
                  wspec(wa), wspec(wb), wspec(wc)],
        out_specs=pl.BlockSpec((tm, d), lambda i: (i, 0)),
        out_shape=jax.ShapeDtypeStruct((n, d), MXU_DTYPE),
        compiler_params=pltpu.CompilerParams(
            dimension_semantics=("arbitrary",), vmem_limit_bytes=VMEM_LIMIT),
        name="merge",
    )(ya, yb, yc, proj, proj, proj, wa, wb, wc)


def _outproj_kernel(m_ref, x_ref, w_ref, fw_ref, o_ref, *, final_norm):
    y = x_ref[...] + jnp.dot(m_ref[...], w_ref[...], preferred_element_type=F32)
    if final_norm:
        ms = jnp.mean(y * y, axis=-1, keepdims=True)
        y = y * lax.rsqrt(ms + EPS) * fw_ref[...]
    o_ref[...] = y


def _outproj(merged, x2d, w_out, final_w, final_norm, tm):
    n, d = x2d.shape
    kern = functools.partial(_outproj_kernel, final_norm=final_norm)
    return pl.pallas_call(
        kern,
        grid=(n // tm,),
        in_specs=[
            pl.BlockSpec((tm, d), lambda i: (i, 0)),
            pl.BlockSpec((tm, d), lambda i: (i, 0)),
            pl.BlockSpec((d, d), lambda i: (0, 0)),
            pl.BlockSpec((1, d), lambda i: (0, 0)),
        ],
        out_specs=pl.BlockSpec((tm, d), lambda i: (i, 0)),
        out_shape=jax.ShapeDtypeStruct((n, d), F32),
        compiler_params=pltpu.CompilerParams(
            dimension_semantics=("arbitrary",), vmem_limit_bytes=VMEM_LIMIT),
        name="outproj",
    )(merged, x2d, w_out, final_w.reshape(1, d))


def _wprep_kernel(a_ref, b_ref, o_ref, *, used):
    j = pl.program_id(1)
    nfirst = COL_UB // COL_TILE
    col = j * COL_TILE + lax.broadcasted_iota(jnp.int32, o_ref.shape, 1)

    @pl.when(j < nfirst)
    def _():
        o_ref[...] = jnp.where(col < FIRST_RUN, a_ref[...].T, 0.0).astype(o_ref.dtype)

    @pl.when(j >= nfirst)
    def _():
        shift = FIRST_RUN - (COL_UB - COL_TILE)
        win = jnp.concatenate([a_ref[...], b_ref[...]], axis=0)[shift:shift + COL_TILE]
        o_ref[...] = jnp.where(col < used, win.T, 0.0).astype(o_ref.dtype)


def _prepare_w_in(w_in, d, conv_ch, kb):
    assert conv_ch == CONV_CH and w_in.shape[-1] == FIRST_RUN + (COL_GATES - COL_UB) + N_BRANCH * d
    assert (FIRST_RUN - COL_UB) % SUBLANES == 0
    depth = w_in.shape[0]
    used = COL_GATES + N_BRANCH * d
    np_ = -(-used // COL_TILE) * COL_TILE
    nfirst = COL_UB // COL_TILE
    w_t = jnp.swapaxes(w_in, 1, 2)
    src = lambda back: pl.BlockSpec(
        (None, COL_TILE, kb), lambda l, j, r: (l, jnp.where(j < nfirst, j, j - back), r))
    return pl.pallas_call(
        functools.partial(_wprep_kernel, used=used),
        grid=(depth, np_ // COL_TILE, d // kb),
        in_specs=[src(1), src(0)],
        out_specs=pl.BlockSpec((None, kb, COL_TILE), lambda l, j, r: (l, r, j)),
        out_shape=jax.ShapeDtypeStruct((depth, d, np_), MXU_DTYPE),
        compiler_params=pltpu.CompilerParams(
            dimension_semantics=("arbitrary", "arbitrary", "arbitrary"), vmem_limit_bytes=VMEM_LIMIT),
        name="wprep",
    )(w_t, w_t)


def _rope_tables(seq):
    inv = 1.0 / (ROPE_THETA ** (jnp.arange(0, HEAD_DIM, 2, dtype=F32) / HEAD_DIM))
    ang = jnp.arange(seq, dtype=F32)[:, None] * inv[None, :]
    cos, sin = jnp.cos(ang), jnp.sin(ang)
    return jnp.concatenate([cos, cos], axis=1), jnp.concatenate([-sin, sin], axis=1)


def kernel(x, norm_w, w_in, conv_w, conv_b, conv_ln_w, conv_ln_b, w_a_out, w_b_out, w_c_out, w_out, final_norm_w):
    batch, seq, d = x.shape
    depth = norm_w.shape[0]
    conv_ch = conv_b.shape[-1]
    n = batch * seq
    assert seq % BLOCK_Q == 0 and conv_ch == COL_TILE and conv_w.shape[1] - 1 <= CONV_HIST
    top_k = min(IDX_TOPK, seq // 4)
    kc = 256
    assert seq % kc == 0 and COL_GATES % d == 0

    w_prep = _prepare_w_in(w_in, d, conv_ch, min(512, d))
    wa, wb, wc, wo = (w.astype(MXU_DTYPE) for w in (w_a_out, w_b_out, w_c_out, w_out))
    cw = conv_w.reshape(depth, conv_w.shape[1], conv_ch)
    cos_t, sin_t = _rope_tables(seq)

    tm_in = min(1024, n)
    ts_prep = min(256, seq)
    ts_conv = min(512, seq)
    tq_dsa = min(256, seq)
    tq_sb = min(2048, seq)
    assert tq_dsa % kc == 0 and tq_sb % kc == 0 and seq % tq_sb == 0

    x2d = x.reshape(n, d)
    for layer in range(depth):
        proj = _inproj(x2d, norm_w[layer], w_prep, layer, tm_in, COL_TILE)
        qi, qa, qc, kcs, vcs, ka, va, ki, wi = _prep(proj, cos_t, sin_t, batch, seq, ts_prep)
        ya = _dsa(qi, ki, wi, qa, ka, va, proj, batch, seq, top_k, tq_dsa, kc)
        yc = _sb(qc, kcs, vcs, proj, batch, seq, tq_sb, kc)
        yb = _conv(proj, cw[layer], conv_b[layer], conv_ln_w[layer], conv_ln_b[layer], batch, seq, ts_conv, 32)
        merged = _merge(ya, yb, yc, proj, wa[layer], wb[layer], wc[layer], COL_GATES, min(256, n))
        x2d = _outproj(merged, x2d, wo[layer], final_norm_w, layer == depth - 1, min(512, n))
    return x2d.reshape(batch, seq, d)
```

```python
import functools

import jax
import jax.numpy as jnp
from jax import lax
from jax.experimental import pallas as pl
from jax.experimental.pallas import tpu as pltpu

HEAD_DIM = 128
ROPE_THETA = 10000.0
BLOCK_Q = 128
EPS = 1e-6
NEG_INF = -1e30
MASK_BIAS = -2e30
A_HEADS = 8
A_KV_HEADS = 2
IDX_HEADS = 16
IDX_TOPK = 256
C_HEADS = 8
N_BRANCH = 3
LANES = 128
SUBLANES = 8
LOG2E = 1.4426950408889634
INT_MIN = -(2 ** 31)

MXU_DTYPE = jnp.bfloat16
F32 = jnp.float32

A_WIDTH = A_HEADS * HEAD_DIM
A_KV_WIDTH = A_KV_HEADS * HEAD_DIM
IDX_WIDTH = IDX_HEADS * HEAD_DIM
C_WIDTH = C_HEADS * HEAD_DIM

COL_TILE = 1024
CONV_CH = COL_TILE
COL_QA = 0
COL_KA = COL_QA + A_WIDTH
COL_VA = COL_KA + A_KV_WIDTH
COL_ZA = COL_VA + A_KV_WIDTH
COL_QI = COL_ZA + A_WIDTH
COL_KI = COL_QI + IDX_WIDTH
COL_WI = COL_KI + HEAD_DIM
FIRST_RUN = COL_WI + IDX_HEADS
COL_UB = -(-FIRST_RUN // COL_TILE) * COL_TILE
COL_ZB = COL_UB + 2 * CONV_CH
COL_QC = COL_ZB + CONV_CH
COL_KC = COL_QC + C_WIDTH
COL_VC = COL_KC + C_WIDTH
COL_ZC = COL_VC + C_WIDTH
COL_GATES = COL_ZC + C_WIDTH
HALF_TILE = COL_TILE // 2

VMEM_LIMIT = 56 * 1024 * 1024


def _nt_dot(a, b):
    return lax.dot_general(a, b, (((1,), (1,)), ((), ())), preferred_element_type=F32)


def _silu(x):
    return x * jax.nn.sigmoid(x)


def _inproj_kernel(x_ref, nw_ref, w_ref, o_ref, h_ref):
    @pl.when(pl.program_id(1) == 0)
    def _():
        xf = x_ref[...]
        ms = jnp.mean(xf * xf, axis=-1, keepdims=True)
        h_ref[...] = (xf * lax.rsqrt(ms + EPS) * nw_ref[...]).astype(h_ref.dtype)

    o_ref[...] = jnp.dot(h_ref[...], w_ref[...], preferred_element_type=F32)


def _inproj(x2d, norm_w, w_prep, layer, tm, tn):
    n, d = x2d.shape
    np_ = w_prep.shape[2]
    return pl.pallas_call(
        _inproj_kernel,
        grid=(n // tm, np_ // tn),
        in_specs=[
            pl.BlockSpec((tm, d), lambda i, j: (i, 0)),
            pl.BlockSpec((1, d), lambda i, j: (0, 0)),
            pl.BlockSpec((None, d, tn), lambda i, j: (layer, 0, j)),
        ],
        out_specs=pl.BlockSpec((tm, tn), lambda i, j: (i, j)),
        out_shape=jax.ShapeDtypeStruct((n, np_), F32),
        scratch_shapes=[pltpu.VMEM((tm, d), MXU_DTYPE)],
        compiler_params=pltpu.CompilerParams(
            dimension_semantics=("arbitrary", "arbitrary"), vmem_limit_bytes=VMEM_LIMIT),
        name="inproj",
    )(x2d, norm_w.reshape(1, d), w_prep)


def _rope(x, cos, sin_signed):
    return x * cos + pltpu.roll(x, HEAD_DIM // 2, axis=1) * sin_signed


def _prep_kernel(qa_ref, kva_ref, qi0_ref, qi1_ref, qi2_ref, qi3_ref, ki_ref, wi_ref, qc_ref, kc_ref, vc_ref,
                 cos_ref, sin_ref,
                 qi_o, qa_o, qc_o, kc_o, vc_o, ka_o, va_o, ki_o, wi_o, *, qk_scale, wi_scale):
    cos = cos_ref[...]
    sin = sin_ref[...]
    heads_per_half = HALF_TILE // HEAD_DIM
    for t, src in enumerate((qi0_ref, qi1_ref, qi2_ref, qi3_ref)):
        for h in range(heads_per_half):
            x = src[:, h * HEAD_DIM:(h + 1) * HEAD_DIM]
            qi_o[0, t * heads_per_half + h] = _rope(x, cos, sin).astype(qi_o.dtype)
    for h in range(A_HEADS):
        x = qa_ref[:, h * HEAD_DIM:(h + 1) * HEAD_DIM]
        qa_o[0, h] = (_rope(x, cos, sin) * qk_scale).astype(qa_o.dtype)
    for h in range(C_HEADS):
        sl = slice(h * HEAD_DIM, (h + 1) * HEAD_DIM)
        qc_o[0, h] = (qc_ref[:, sl] * qk_scale).astype(qc_o.dtype)
        kc_o[0, h] = kc_ref[:, sl].astype(kc_o.dtype)
        vc_o[0, h] = vc_ref[:, sl].astype(vc_o.dtype)
    for h in range(A_KV_HEADS):
        ka_o[0, h] = _rope(kva_ref[:, h * HEAD_DIM:(h + 1) * HEAD_DIM], cos, sin).astype(ka_o.dtype)
        va_o[0, h] = kva_ref[:, A_KV_WIDTH + h * HEAD_DIM:A_KV_WIDTH + (h + 1) * HEAD_DIM].astype(va_o.dtype)
    ki_o[0] = _rope(ki_ref[...], cos, sin).astype(ki_o.dtype)
    wi_o[...] = wi_ref[...] * wi_scale[0] * wi_scale[1]


def _prep(proj, cos_t, sin_t, batch, seq, ts):
    n = proj.shape[0]
    nst = seq // ts
    row = lambda b, s: b * nst + s
    col = lambda off, w: pl.BlockSpec((ts, w), lambda b, s: (row(b, s), off // w))
    tab = pl.BlockSpec((ts, HEAD_DIM), lambda b, s: (s, 0))
    in_cols = ([col(COL_QA, A_WIDTH), col(COL_KA, 2 * A_KV_WIDTH)]
               + [col(COL_QI + t * HALF_TILE, HALF_TILE) for t in range(IDX_WIDTH // HALF_TILE)]
               + [col(COL_KI, HEAD_DIM), col(COL_WI, LANES),
                  col(COL_QC, C_WIDTH), col(COL_KC, C_WIDTH), col(COL_VC, C_WIDTH)])
    heads = lambda nh: pl.BlockSpec((1, nh, ts, HEAD_DIM), lambda b, s: (b, 0, s, 0))
    hshape = lambda nh: jax.ShapeDtypeStruct((batch, nh, seq, HEAD_DIM), MXU_DTYPE)
    kern = functools.partial(
        _prep_kernel, qk_scale=HEAD_DIM ** -0.5 * LOG2E, wi_scale=(IDX_HEADS ** -0.5, HEAD_DIM ** -0.5))
    return pl.pallas_call(
        kern,
        grid=(batch, nst),
        in_specs=in_cols + [tab, tab],
        out_specs=[
            heads(IDX_HEADS), heads(A_HEADS), heads(C_HEADS), heads(C_HEADS), heads(C_HEADS),
            heads(A_KV_HEADS), heads(A_KV_HEADS),
            pl.BlockSpec((1, ts, HEAD_DIM), lambda b, s: (b, s, 0)),
            pl.BlockSpec((ts, LANES), lambda b, s: (row(b, s), 0)),
        ],
        out_shape=[
            hshape(IDX_HEADS), hshape(A_HEADS), hshape(C_HEADS), hshape(C_HEADS), hshape(C_HEADS),
            hshape(A_KV_HEADS), hshape(A_KV_HEADS),
            jax.ShapeDtypeStruct((batch, seq, HEAD_DIM), MXU_DTYPE),
            jax.ShapeDtypeStruct((n, LANES), F32),
        ],
        compiler_params=pltpu.CompilerParams(
            dimension_semantics=("arbitrary", "arbitrary"), vmem_limit_bytes=VMEM_LIMIT),
        name="prep",
    )(*([proj] * len(in_cols)), cos_t, sin_t)


def _sortable_key(x):
    bits = lax.bitcast_convert_type(x, jnp.int32)
    return jnp.where(bits < 0, bits ^ jnp.int32(0x7FFFFFFF), bits)


IDX_HEAD_GROUP = 4


PACK_ROWS = 16
HALF_OFFSET = 32768


def _dsa_kernel(qi_ref, ki_ref, wi_ref, qa_ref, ka_ref, va_ref, za0_ref, za1_ref, o_ref,
                key_ref, hi_ref, lo_ref, wb_ref, thr_ref, m_ref, l_ref, acc_ref, *, top_k, tq, kc):
    i = pl.program_id(1)
    nch = ((i + 1) * tq) // kc
    t_idx = i * tq + lax.broadcasted_iota(jnp.int32, (tq, kc), 0)
    lane = lax.broadcasted_iota(jnp.int32, (tq, kc), 1)
    rep = kc // LANES

    wi = wi_ref[...]
    for h in range(IDX_HEADS):
        wb_ref[h] = jnp.broadcast_to(wi[:, h:h + 1], (tq, kc))

    def score_body(c, carry):
        k0 = pl.multiple_of(c * kc, kc)
        kch = ki_ref[0, pl.ds(k0, kc), :]
        acc = None
        for g in range(IDX_HEADS // IDX_HEAD_GROUP):
            qg = qi_ref[0, g * IDX_HEAD_GROUP:(g + 1) * IDX_HEAD_GROUP].reshape(IDX_HEAD_GROUP * tq, HEAD_DIM)
            rel = _nt_dot(qg, kch)
            for hh in range(IDX_HEAD_GROUP):
                term = wb_ref[g * IDX_HEAD_GROUP + hh] * jnp.maximum(rel[hh * tq:(hh + 1) * tq], 0.0)
                acc = term if acc is None else acc + term
        score = jnp.where(k0 + lane <= t_idx, acc, NEG_INF)
        key = _sortable_key(score)
        key_ref[:, pl.ds(k0, kc)] = key
        key_t = key.T
        hi_ref[pl.ds(k0, kc), :] = lax.shift_right_arithmetic(key_t, 16).astype(jnp.int16)
        lo_ref[pl.ds(k0, kc), :] = ((key_t & 0xFFFF) - HALF_OFFSET).astype(jnp.int16)
        return carry

    lax.fori_loop(0, nch, score_body, 0)

    need_bisect = (i + 1) * tq > top_k

    @pl.when(jnp.logical_not(need_bisect))
    def _():
        thr_ref[...] = jnp.full(thr_ref.shape, INT_MIN, jnp.int32)

    @pl.when(need_bisect)
    def _():
        zero = jnp.zeros((PACK_ROWS, tq), jnp.int16)

        def count(ref, pred):
            def body(c, cnt):
                v = ref[pl.ds(pl.multiple_of(c * kc, kc), kc), :]
                ind = jnp.where(pred(v), jnp.int16(1), jnp.int16(0))
                for r in range(kc // PACK_ROWS):
                    cnt = cnt + ind[r * PACK_ROWS:(r + 1) * PACK_ROWS]
                return cnt
            cnt = lax.fori_loop(0, nch, body, zero)
            return jnp.sum(cnt.astype(F32), axis=0, keepdims=True)

        def bisect16(ref, need):
            def bit_body(b, u):
                ub = u | lax.shift_left(jnp.int32(1), 15 - b)
                cand = jnp.broadcast_to((ub - HALF_OFFSET).astype(jnp.int16), (kc, tq))
                tot = count(ref, lambda v: v >= cand)
                return jnp.where(tot >= need, ub, u)
            return lax.fori_loop(0, 16, bit_body, jnp.zeros((1, tq), jnp.int32))

        u_hi = bisect16(hi_ref, float(top_k))
        t_hi = jnp.broadcast_to((u_hi - HALF_OFFSET).astype(jnp.int16), (kc, tq))
        above = count(hi_ref, lambda v: v > t_hi)

        def mask_body(c, carry):
            sl = pl.ds(pl.multiple_of(c * kc, kc), kc)
            lo_ref[sl, :] = jnp.where(hi_ref[sl, :] == t_hi, lo_ref[sl, :], jnp.int16(-HALF_OFFSET))
            return carry

        lax.fori_loop(0, nch, mask_body, 0)
        u_lo = bisect16(lo_ref, float(top_k) - above)
        thr_q = (u_hi - HALF_OFFSET) * (2 * HALF_OFFSET) + u_lo
        t_q = i * tq + lax.broadcasted_iota(jnp.int32, (1, tq), 1)
        thr_q = jnp.where(t_q < top_k, jnp.int32(INT_MIN), thr_q)
        for r in range(tq // LANES):
            blk = jnp.broadcast_to(thr_q[:, r * LANES:(r + 1) * LANES], (LANES, LANES))
            thr_ref[r * LANES:(r + 1) * LANES, :] = blk.T

    thr_k = jnp.concatenate([thr_ref[...]] * rep, axis=1)
    group = A_HEADS // A_KV_HEADS
    m_ref[...] = jnp.full(m_ref.shape, NEG_INF, F32)
    l_ref[...] = jnp.zeros(l_ref.shape, F32)
    acc_ref[...] = jnp.zeros(acc_ref.shape, F32)

    def att_body(c, carry):
        k0 = pl.multiple_of(c * kc, kc)
        kk = key_ref[:, pl.ds(k0, kc)]
        bias = jnp.where(k0 + lane <= t_idx, jnp.where(kk >= thr_k, 0.0, MASK_BIAS), MASK_BIAS)
        bias = jnp.concatenate([bias] * group, axis=0)
        for c2 in range(A_KV_HEADS):
            q = qa_ref[0, c2 * group:(c2 + 1) * group].reshape(group * tq, HEAD_DIM)
            s = _nt_dot(q, ka_ref[0, c2, pl.ds(k0, kc), :]) + bias
            m_prev = m_ref[c2]
            m_next = jnp.maximum(m_prev, jnp.max(s, axis=1, keepdims=True))
            p = jnp.exp2(s - jnp.concatenate([m_next] * rep, axis=1))
            alpha = jnp.exp2(m_prev - m_next)
            l_ref[c2] = alpha * l_ref[c2] + jnp.sum(p, axis=1, keepdims=True)
            m_ref[c2] = m_next
            acc_ref[c2] = alpha * acc_ref[c2] + jnp.dot(
                p.astype(MXU_DTYPE), va_ref[0, c2, pl.ds(k0, kc), :], preferred_element_type=F32)
        return carry

    lax.fori_loop(0, nch, att_body, 0)
    heads_per_half = HALF_TILE // HEAD_DIM
    for c2 in range(A_KV_HEADS):
        for g in range(group):
            h = c2 * group + g
            rows = slice(g * tq, (g + 1) * tq)
            out = acc_ref[c2, rows, :] / l_ref[c2, rows, :]
            z_ref = (za0_ref, za1_ref)[h // heads_per_half]
            za = z_ref[:, (h % heads_per_half) * HEAD_DIM:(h % heads_per_half + 1) * HEAD_DIM]
            o_ref[:, h * HEAD_DIM:(h + 1) * HEAD_DIM] = (out * _silu(za)).astype(o_ref.dtype)


def _dsa(qi, ki, wi, qa, ka, va, proj, batch, seq, top_k, tq, kc):
    nqt = seq // tq
    n = batch * seq
    row = lambda b, i: b * nqt + i
    group = A_HEADS // A_KV_HEADS
    kern = functools.partial(_dsa_kernel, top_k=top_k, tq=tq, kc=kc)
    stat = pltpu.VMEM((A_KV_HEADS, group * tq, HEAD_DIM), F32)
    return pl.pallas_call(
        kern,
        grid=(batch, nqt),
        in_specs=[
            pl.BlockSpec((1, IDX_HEADS, tq, HEAD_DIM), lambda b, i: (b, 0, i, 0)),
            pl.BlockSpec((1, seq, HEAD_DIM), lambda b, i: (b, 0, 0)),
            pl.BlockSpec((tq, LANES), lambda b, i: (row(b, i), 0)),
            pl.BlockSpec((1, A_HEADS, tq, HEAD_DIM), lambda b, i: (b, 0, i, 0)),
            pl.BlockSpec((1, A_KV_HEADS, seq, HEAD_DIM), lambda b, i: (b, 0, 0, 0)),
            pl.BlockSpec((1, A_KV_HEADS, seq, HEAD_DIM), lambda b, i: (b, 0, 0, 0)),
            pl.BlockSpec((tq, HALF_TILE), lambda b, i: (row(b, i), COL_ZA // HALF_TILE)),
            pl.BlockSpec((tq, HALF_TILE), lambda b, i: (row(b, i), COL_ZA // HALF_TILE + 1)),
        ],
        out_specs=pl.BlockSpec((tq, A_WIDTH), lambda b, i: (row(b, i), 0)),
        out_shape=jax.ShapeDtypeStruct((n, A_WIDTH), MXU_DTYPE),
        scratch_shapes=[
            pltpu.VMEM((tq, seq), jnp.int32),
            pltpu.VMEM((seq, tq), jnp.int16),
            pltpu.VMEM((seq, tq), jnp.int16),
            pltpu.VMEM((IDX_HEADS, tq, kc), F32),
            pltpu.VMEM((tq, LANES), jnp.int32),
            stat, stat, stat,
        ],
        compiler_params=pltpu.CompilerParams(
            dimension_semantics=("arbitrary", "arbitrary"), vmem_limit_bytes=VMEM_LIMIT),
        name="dsa",
    )(qi, ki, wi, qa, ka, va, proj, proj)


SB_SUB = 128
SB_DEAD_LOG2 = -160.0


def _sb_chunk(q, kch, vch, tri2, rests, row0, key0, kc):
    nsub = q.shape[0] // SB_SUB
    z = _nt_dot(q, kch)
    his, los, lbs, sums, masks = [], [], [], [], []
    for s in range(nsub):
        zs = z[s * SB_SUB:(s + 1) * SB_SUB]
        m0 = jnp.minimum(zs, 0.0)
        t1 = m0 - zs
        sp = jnp.log2(1.0 + jnp.exp2(m0 + t1))
        log_beta = m0 - sp
        log_keep = t1 - sp
        mask = None
        first_row = row0 + s * SB_SUB
        if key0 is not None and key0 + kc - 1 >= first_row:
            mask = (key0 + lax.broadcasted_iota(jnp.int32, (SB_SUB, kc), 1)
                    < first_row + lax.broadcasted_iota(jnp.int32, (SB_SUB, kc), 0))
            log_keep = jnp.where(mask, log_keep, 0.0)
        hi = log_keep.astype(MXU_DTYPE)
        his.append(hi)
        los.append((log_keep - hi.astype(F32)).astype(MXU_DTYPE))
        lbs.append(log_beta)
        masks.append(mask)
        sums.append(jnp.sum(log_keep, axis=1, keepdims=True))
    hl = jnp.concatenate([jnp.concatenate(his, axis=0), jnp.concatenate(los, axis=0)], axis=1)
    after = jnp.dot(hl, tri2, preferred_element_type=F32)
    es = []
    for s in range(nsub):
        e = jnp.exp2(lbs[s] + after[s * SB_SUB:(s + 1) * SB_SUB] + rests[s])
        if masks[s] is not None:
            e = jnp.where(masks[s], e, 0.0)
        es.append(e.astype(MXU_DTYPE))
    pv = jnp.dot(jnp.concatenate(es, axis=0), vch, preferred_element_type=F32)
    return [r + sm for r, sm in zip(rests, sums)], pv


def _sb_kernel(q_ref, k_ref, v_ref, z_ref, o_ref, *, tq, kc):
    i = pl.program_id(2)
    nsub = tq // SB_SUB
    ndiag = tq // kc
    tri = jnp.where(lax.broadcasted_iota(jnp.int32, (kc, kc), 0) > lax.broadcasted_iota(jnp.int32, (kc, kc), 1),
                    1.0, 0.0).astype(MXU_DTYPE)
    tri2 = jnp.concatenate([tri, tri], axis=0)
    q = q_ref[0, 0]

    def kv(k0):
        k0 = pl.multiple_of(k0, kc)
        return k_ref[0, 0, pl.ds(k0, kc), :], v_ref[0, 0, pl.ds(k0, kc), :]

    rests = [jnp.zeros((SB_SUB, 1), F32) for _ in range(nsub)]
    acc = jnp.zeros((tq, HEAD_DIM), F32)
    for d in reversed(range(ndiag)):
        row0 = d * kc
        s0 = row0 // SB_SUB
        kch, vch = kv(i * tq + d * kc)
        new, pv = _sb_chunk(q[row0:], kch, vch, tri2, rests[s0:], row0, d * kc, kc)
        rests = rests[:s0] + new
        acc = jnp.concatenate([acc[:row0], acc[row0:] + pv], axis=0) if row0 else acc + pv

    nfull = i * ndiag

    def alive(rests):
        top = rests[0]
        for r in rests[1:]:
            top = jnp.maximum(top, r)
        return (jnp.max(top) > SB_DEAD_LOG2).astype(jnp.int32)

    def cond(carry):
        return jnp.logical_and(carry[0] < nfull, carry[1] > 0)

    def body(carry):
        j, _, rests, acc = carry
        kch, vch = kv((nfull - 1 - j) * kc)
        rests, pv = _sb_chunk(q, kch, vch, tri2, list(rests), 0, None, kc)
        return j + 1, alive(rests), tuple(rests), acc + pv

    acc = lax.while_loop(cond, body, (jnp.int32(0), alive(rests), tuple(rests), acc))[3]
    o_ref[...] = (acc * _silu(z_ref[...])).astype(o_ref.dtype)


def _sb(qc, kc_, vc, proj, batch, seq, tq, kc):
    nqb = seq // tq
    n = batch * seq
    col_zc = COL_ZC // HEAD_DIM
    kern = functools.partial(_sb_kernel, tq=tq, kc=kc)
    return pl.pallas_call(
        kern,
        grid=(batch, C_HEADS, nqb),
        in_specs=[
            pl.BlockSpec((1, 1, tq, HEAD_DIM), lambda b, h, i: (b, h, i, 0)),
            pl.BlockSpec((1, 1, seq, HEAD_DIM), lambda b, h, i: (b, h, 0, 0)),
            pl.BlockSpec((1, 1, seq, HEAD_DIM), lambda b, h, i: (b, h, 0, 0)),
            pl.BlockSpec((tq, HEAD_DIM), lambda b, h, i: (b * nqb + i, col_zc + h)),
        ],
        out_specs=pl.BlockSpec((tq, HEAD_DIM), lambda b, h, i: (b * nqb + i, h)),
        out_shape=jax.ShapeDtypeStruct((n, C_WIDTH), MXU_DTYPE),
        compiler_params=pltpu.CompilerParams(
            dimension_semantics=("arbitrary", "arbitrary", "arbitrary"), vmem_limit_bytes=VMEM_LIMIT),
        name="sb",
    )(qc, kc_, vc, proj)


CONV_HIST = 32


def _conv_kernel(ua_ref, ug_ref, zb_ref, cw_ref, cb_ref, lw_ref, lb_ref, o_ref, ysh, *, ts, width, rc):
    ch = o_ref.shape[1]

    @pl.when(pl.program_id(1) == 0)
    def _():
        ysh[0, 0:CONV_HIST, :] = jnp.zeros((CONV_HIST, ch), F32)

    @pl.when(pl.program_id(1) > 0)
    def _():
        ysh[0, 0:CONV_HIST, :] = ysh[0, ts:ts + CONV_HIST, :]

    ysh[0, CONV_HIST:CONV_HIST + ts, :] = ua_ref[...] * jax.nn.sigmoid(ug_ref[...])
    span = CONV_HIST + ts - SUBLANES
    for r in range(1, SUBLANES):
        ysh[r, 0:span, :] = ysh[0, r:r + span, :]

    def chunk(c, carry):
        r0 = pl.multiple_of(c * rc, rc)
        acc = jnp.broadcast_to(cb_ref[...], (rc, ch))
        for j in range(width):
            off = CONV_HIST - (width - 1) + j
            r = off % SUBLANES
            acc = acc + cw_ref[j:j + 1, :] * ysh[r, pl.ds(r0 + (off - r), rc), :]
        mu = jnp.mean(acc, axis=1, keepdims=True)
        d = acc - mu
        var = jnp.mean(d * d, axis=1, keepdims=True)
        y = d * lax.rsqrt(var + EPS) * lw_ref[...] + lb_ref[...]
        o_ref[pl.ds(r0, rc), :] = (_silu(y) * _silu(zb_ref[pl.ds(r0, rc), :])).astype(o_ref.dtype)
        return carry

    lax.fori_loop(0, ts // rc, chunk, 0)


def _conv(proj, conv_w, conv_b, ln_w, ln_b, batch, seq, ts, rc):
    n = batch * seq
    width, ch = conv_w.shape
    nst = seq // ts
    vec = pl.BlockSpec((1, ch), lambda b, s: (0, 0))
    kern = functools.partial(_conv_kernel, ts=ts, width=width, rc=rc)
    return pl.pallas_call(
        kern,
        grid=(batch, nst),
        in_specs=[
            pl.BlockSpec((ts, ch), lambda b, s: (b * nst + s, COL_UB // ch)),
            pl.BlockSpec((ts, ch), lambda b, s: (b * nst + s, COL_UB // ch + 1)),
            pl.BlockSpec((ts, ch), lambda b, s: (b * nst + s, COL_ZB // ch)),
            pl.BlockSpec((width, ch), lambda b, s: (0, 0)),
            vec, vec, vec,
        ],
        out_specs=pl.BlockSpec((ts, ch), lambda b, s: (b * nst + s, 0)),
        out_shape=jax.ShapeDtypeStruct((n, ch), MXU_DTYPE),
        scratch_shapes=[pltpu.VMEM((SUBLANES, CONV_HIST + ts, ch), F32)],
        compiler_params=pltpu.CompilerParams(
            dimension_semantics=("arbitrary", "arbitrary"), vmem_limit_bytes=VMEM_LIMIT),
        name="conv",
    )(proj, proj, proj, conv_w, conv_b.reshape(1, ch), ln_w.reshape(1, ch), ln_b.reshape(1, ch))


def _merge_kernel(ya_ref, yb_ref, yc_ref, g0_ref, g1_ref, g2_ref, wa_ref, wb_ref, wc_ref, o_ref):
    ya = jnp.dot(ya_ref[...], wa_ref[...], preferred_element_type=F32)
    yb = jnp.dot(yb_ref[...], wb_ref[...], preferred_element_type=F32)
    yc = jnp.dot(yc_ref[...], wc_ref[...], preferred_element_type=F32)
    merged = (jax.nn.sigmoid(g0_ref[...]) * ya + jax.nn.sigmoid(g1_ref[...]) * yb
              + jax.nn.sigmoid(g2_ref[...]) * yc)
    o_ref[...] = merged.astype(o_ref.dtype)


def _merge(ya, yb, yc, proj, wa, wb, wc, col_gates, tm):
    n = ya.shape[0]
    d = wa.shape[1]
    act = lambda w: pl.BlockSpec((tm, w), lambda i: (i, 0))
    gate = lambda k: pl.BlockSpec((tm, d), lambda i, k=k: (i, col_gates // d + k))
    wspec = lambda w: pl.BlockSpec(w.shape, lambda i: (0, 0))
    return pl.pallas_call(
        _merge_kernel,
        grid=(n // tm,),
        in_specs=[act(ya.shape[1]), act(yb.shape[1]), act(yc.shape[1]), gate(0), gate(1), gate(2),
                  wspec(wa), wspec(wb), wspec(wc)],
        out_specs=pl.BlockSpec((tm, d), lambda i: (i, 0)),
        out_shape=jax.ShapeDtypeStruct((n, d), MXU_DTYPE),
        compiler_params=pltpu.CompilerParams(
            dimension_semantics=("arbitrary",), vmem_limit_bytes=VMEM_LIMIT),
        name="merge",
    )(ya, yb, yc, proj, proj, proj, wa, wb, wc)


def _outproj_kernel(m_ref, x_ref, w_ref, fw_ref, o_ref, *, final_norm):
    y = x_ref[...] + jnp.dot(m_ref[...], w_ref[...], preferred_element_type=F32)
    if final_norm:
        ms = jnp.mean(y * y, axis=-1, keepdims=True)
        y = y * lax.rsqrt(ms + EPS) * fw_ref[...]
    o_ref[...] = y


def _outproj(merged, x2d, w_out, final_w, final_norm, tm):
    n, d = x2d.shape
    kern = functools.partial(_outproj_kernel, final_norm=final_norm)
    return pl.pallas_call(
        kern,
        grid=(n // tm,),
        in_specs=[
            pl.BlockSpec((tm, d), lambda i: (i, 0)),
            pl.BlockSpec((tm, d), lambda i: (i, 0)),
            pl.BlockSpec((d, d), lambda i: (0, 0)),
            pl.BlockSpec((1, d), lambda i: (0, 0)),
        ],
        out_specs=pl.BlockSpec((tm, d), lambda i: (i, 0)),
        out_shape=jax.ShapeDtypeStruct((n, d), F32),
        compiler_params=pltpu.CompilerParams(
            dimension_semantics=("arbitrary",), vmem_limit_bytes=VMEM_LIMIT),
        name="outproj",
    )(merged, x2d, w_out, final_w.reshape(1, d))


def _wprep_kernel(a_ref, b_ref, o_ref, *, used):
    j = pl.program_id(1)
    nfirst = COL_UB // COL_TILE
    col = j * COL_TILE + lax.broadcasted_iota(jnp.int32, o_ref.shape, 1)

    @pl.when(j < nfirst)
    def _():
        o_ref[...] = jnp.where(col < FIRST_RUN, a_ref[...].T, 0.0).astype(o_ref.dtype)

    @pl.when(j >= nfirst)
    def _():
        shift = FIRST_RUN - (COL_UB - COL_TILE)
        win = jnp.concatenate([a_ref[...], b_ref[...]], axis=0)[shift:shift + COL_TILE]
        o_ref[...] = jnp.where(col < used, win.T, 0.0).astype(o_ref.dtype)


def _prepare_w_in(w_in, d, conv_ch, kb):
    assert conv_ch == CONV_CH and w_in.shape[-1] == FIRST_RUN + (COL_GATES - COL_UB) + N_BRANCH * d
    assert (FIRST_RUN - COL_UB) % SUBLANES == 0
    depth = w_in.shape[0]
    used = COL_GATES + N_BRANCH * d
    np_ = -(-used // COL_TILE) * COL_TILE
    nfirst = COL_UB // COL_TILE
    w_t = jnp.swapaxes(w_in, 1, 2)
    src = lambda back: pl.BlockSpec(
        (None, COL_TILE, kb), lambda l, j, r: (l, jnp.where(j < nfirst, j, j - back), r))
    return pl.pallas_call(
        functools.partial(_wprep_kernel, used=used),
        grid=(depth, np_ // COL_TILE, d // kb),
        in_specs=[src(1), src(0)],
        out_specs=pl.BlockSpec((None, kb, COL_TILE), lambda l, j, r: (l, r, j)),
        out_shape=jax.ShapeDtypeStruct((depth, d, np_), MXU_DTYPE),
        compiler_params=pltpu.CompilerParams(
            dimension_semantics=("arbitrary", "arbitrary", "arbitrary"), vmem_limit_bytes=VMEM_LIMIT),
        name="wprep",
    )(w_t, w_t)


def _rope_tables(seq):
    inv = 1.0 / (ROPE_THETA ** (jnp.arange(0, HEAD_DIM, 2, dtype=F32) / HEAD_DIM))
    ang = jnp.arange(seq, dtype=F32)[:, None] * inv[None, :]
    cos, sin = jnp.cos(ang), jnp.sin(ang)
    return jnp.concatenate([cos, cos], axis=1), jnp.concatenate([-sin, sin], axis=1)


def kernel(x, norm_w, w_in, conv_w, conv_b, conv_ln_w, conv_ln_b, w_a_out, w_b_out, w_c_out, w_out, final_norm_w):
    batch, seq, d = x.shape
    depth = norm_w.shape[0]
    conv_ch = conv_b.shape[-1]
    n = batch * seq
    assert seq % BLOCK_Q == 0 and conv_ch == COL_TILE and conv_w.shape[1] - 1 <= CONV_HIST
    top_k = min(IDX_TOPK, seq // 4)
    kc = 256
    assert seq % kc == 0 and COL_GATES % d == 0

    w_prep = _prepare_w_in(w_in, d, conv_ch, min(512, d))
    wa, wb, wc, wo = (w.astype(MXU_DTYPE) for w in (w_a_out, w_b_out, w_c_out, w_out))
    cw = conv_w.reshape(depth, conv_w.shape[1], conv_ch)
    cos_t, sin_t = _rope_tables(seq)

    tm_in = min(1024, n)
    ts_prep = min(256, seq)
    ts_conv = min(512, seq)
    tq_dsa = min(256, seq)
    tq_sb = min(512, seq)
    assert tq_dsa % kc == 0 and tq_sb % kc == 0 and seq % tq_sb == 0

    x2d = x.reshape(n, d)
    for layer in range(depth):
        proj = _inproj(x2d, norm_w[layer], w_prep, layer, tm_in, COL_TILE)
        qi, qa, qc, kcs, vcs, ka, va, ki, wi = _prep(proj, cos_t, sin_t, batch, seq, ts_prep)
        ya = _dsa(qi, ki, wi, qa, ka, va, proj, batch, seq, top_k, tq_dsa, kc)
        yc = _sb(qc, kcs, vcs, proj, batch, seq, tq_sb, kc)
        yb = _conv(proj, cw[layer], conv_b[layer], conv_ln_w[layer], conv_ln_b[layer], batch, seq, ts_conv, 32)
        merged = _merge(ya, yb, yc, proj, wa[layer], wb[layer], wc[layer], COL_GATES, min(256, n))
        x2d = _outproj(merged, x2d, wo[layer], final_norm_w, layer == depth - 1, min(512, n))
    return x2d.reshape(batch, seq, d)
```

```python
import functools

import jax
import jax.numpy as jnp
from jax import lax
from jax.experimental import pallas as pl
from jax.experimental.pallas import tpu as pltpu

HEAD_DIM = 128
ROPE_THETA = 10000.0
BLOCK_Q = 128
EPS = 1e-6
NEG_INF = -1e30
MASK_BIAS = -2e30
A_HEADS = 8
A_KV_HEADS = 2
IDX_HEADS = 16
IDX_TOPK = 256
C_HEADS = 8
N_BRANCH = 3
LANES = 128
SUBLANES = 8
LOG2E = 1.4426950408889634
INT_MIN = -(2 ** 31)

MXU_DTYPE = jnp.bfloat16
F32 = jnp.float32

A_WIDTH = A_HEADS * HEAD_DIM
A_KV_WIDTH = A_KV_HEADS * HEAD_DIM
IDX_WIDTH = IDX_HEADS * HEAD_DIM
C_WIDTH = C_HEADS * HEAD_DIM

COL_TILE = 1024
CONV_CH = COL_TILE
COL_QA = 0
COL_KA = COL_QA + A_WIDTH
COL_VA = COL_KA + A_KV_WIDTH
COL_ZA = COL_VA + A_KV_WIDTH
COL_QI = COL_ZA + A_WIDTH
COL_KI = COL_QI + IDX_WIDTH
COL_WI = COL_KI + HEAD_DIM
FIRST_RUN = COL_WI + IDX_HEADS
COL_UB = -(-FIRST_RUN // COL_TILE) * COL_TILE
COL_ZB = COL_UB + 2 * CONV_CH
COL_QC = COL_ZB + CONV_CH
COL_KC = COL_QC + C_WIDTH
COL_VC = COL_KC + C_WIDTH
COL_ZC = COL_VC + C_WIDTH
COL_GATES = COL_ZC + C_WIDTH
HALF_TILE = COL_TILE // 2

VMEM_LIMIT = 56 * 1024 * 1024


def _nt_dot(a, b):
    return lax.dot_general(a, b, (((1,), (1,)), ((), ())), preferred_element_type=F32)


def _silu(x):
    return x * jax.nn.sigmoid(x)


def _inproj_kernel(x_ref, nw_ref, w_ref, o_ref, h_ref):
    @pl.when(pl.program_id(1) == 0)
    def _():
        xf = x_ref[...]
        ms = jnp.mean(xf * xf, axis=-1, keepdims=True)
        h_ref[...] = (xf * lax.rsqrt(ms + EPS) * nw_ref[...]).astype(h_ref.dtype)

    o_ref[...] = jnp.dot(h_ref[...], w_ref[...], preferred_element_type=F32)


def _inproj(x2d, norm_w, w_prep, layer, tm, tn):
    n, d = x2d.shape
    np_ = w_prep.shape[2]
    return pl.pallas_call(
        _inproj_kernel,
        grid=(n // tm, np_ // tn),
        in_specs=[
            pl.BlockSpec((tm, d), lambda i, j: (i, 0)),
            pl.BlockSpec((1, d), lambda i, j: (0, 0)),
            pl.BlockSpec((None, d, tn), lambda i, j: (layer, 0, j)),
        ],
        out_specs=pl.BlockSpec((tm, tn), lambda i, j: (i, j)),
        out_shape=jax.ShapeDtypeStruct((n, np_), F32),
        scratch_shapes=[pltpu.VMEM((tm, d), MXU_DTYPE)],
        compiler_params=pltpu.CompilerParams(
            dimension_semantics=("arbitrary", "arbitrary"), vmem_limit_bytes=VMEM_LIMIT),
        name="inproj",
    )(x2d, norm_w.reshape(1, d), w_prep)


def _rope(x, cos, sin_signed):
    return x * cos + pltpu.roll(x, HEAD_DIM // 2, axis=1) * sin_signed


def _prep_kernel(qa_ref, kva_ref, qi0_ref, qi1_ref, qi2_ref, qi3_ref, ki_ref, wi_ref, qc_ref, kc_ref, vc_ref,
                 cos_ref, sin_ref,
                 qi_o, qa_o, qc_o, kc_o, vc_o, ka_o, va_o, ki_o, wi_o, *, qk_scale, wi_scale):
    cos = cos_ref[...]
    sin = sin_ref[...]
    heads_per_half = HALF_TILE // HEAD_DIM
    for t, src in enumerate((qi0_ref, qi1_ref, qi2_ref, qi3_ref)):
        for h in range(heads_per_half):
            x = src[:, h * HEAD_DIM:(h + 1) * HEAD_DIM]
            qi_o[0, t * heads_per_half + h] = _rope(x, cos, sin).astype(qi_o.dtype)
    for h in range(A_HEADS):
        x = qa_ref[:, h * HEAD_DIM:(h + 1) * HEAD_DIM]
        qa_o[0, h] = (_rope(x, cos, sin) * qk_scale).astype(qa_o.dtype)
    for h in range(C_HEADS):
        sl = slice(h * HEAD_DIM, (h + 1) * HEAD_DIM)
        qc_o[0, h] = (qc_ref[:, sl] * qk_scale).astype(qc_o.dtype)
        kc_o[0, h] = kc_ref[:, sl].astype(kc_o.dtype)
        vc_o[0, h] = vc_ref[:, sl].astype(vc_o.dtype)
    for h in range(A_KV_HEADS):
        ka_o[0, h] = _rope(kva_ref[:, h * HEAD_DIM:(h + 1) * HEAD_DIM], cos, sin).astype(ka_o.dtype)
        va_o[0, h] = kva_ref[:, A_KV_WIDTH + h * HEAD_DIM:A_KV_WIDTH + (h + 1) * HEAD_DIM].astype(va_o.dtype)
    ki_o[0] = _rope(ki_ref[...], cos, sin).astype(ki_o.dtype)
    wi_o[...] = wi_ref[...] * wi_scale[0] * wi_scale[1]


def _prep(proj, cos_t, sin_t, batch, seq, ts):
    n = proj.shape[0]
    nst = seq // ts
    row = lambda b, s: b * nst + s
    col = lambda off, w: pl.BlockSpec((ts, w), lambda b, s: (row(b, s), off // w))
    tab = pl.BlockSpec((ts, HEAD_DIM), lambda b, s: (s, 0))
    in_cols = ([col(COL_QA, A_WIDTH), col(COL_KA, 2 * A_KV_WIDTH)]
               + [col(COL_QI + t * HALF_TILE, HALF_TILE) for t in range(IDX_WIDTH // HALF_TILE)]
               + [col(COL_KI, HEAD_DIM), col(COL_WI, LANES),
                  col(COL_QC, C_WIDTH), col(COL_KC, C_WIDTH), col(COL_VC, C_WIDTH)])
    heads = lambda nh: pl.BlockSpec((1, nh, ts, HEAD_DIM), lambda b, s: (b, 0, s, 0))
    hshape = lambda nh: jax.ShapeDtypeStruct((batch, nh, seq, HEAD_DIM), MXU_DTYPE)
    kern = functools.partial(
        _prep_kernel, qk_scale=HEAD_DIM ** -0.5 * LOG2E, wi_scale=(IDX_HEADS ** -0.5, HEAD_DIM ** -0.5))
    return pl.pallas_call(
        kern,
        grid=(batch, nst),
        in_specs=in_cols + [tab, tab],
        out_specs=[
            heads(IDX_HEADS), heads(A_HEADS), heads(C_HEADS), heads(C_HEADS), heads(C_HEADS),
            heads(A_KV_HEADS), heads(A_KV_HEADS),
            pl.BlockSpec((1, ts, HEAD_DIM), lambda b, s: (b, s, 0)),
            pl.BlockSpec((ts, LANES), lambda b, s: (row(b, s), 0)),
        ],
        out_shape=[
            hshape(IDX_HEADS), hshape(A_HEADS), hshape(C_HEADS), hshape(C_HEADS), hshape(C_HEADS),
            hshape(A_KV_HEADS), hshape(A_KV_HEADS),
            jax.ShapeDtypeStruct((batch, seq, HEAD_DIM), MXU_DTYPE),
            jax.ShapeDtypeStruct((n, LANES), F32),
        ],
        compiler_params=pltpu.CompilerParams(
            dimension_semantics=("arbitrary", "arbitrary"), vmem_limit_bytes=VMEM_LIMIT),
        name="prep",
    )(*([proj] * len(in_cols)), cos_t, sin_t)


def _sortable_key(x):
    bits = lax.bitcast_convert_type(x, jnp.int32)
    return jnp.where(bits < 0, bits ^ jnp.int32(0x7FFFFFFF), bits)


IDX_HEAD_GROUP = 1
ATT_MULT = 2


PACK_ROWS = 16
HALF_OFFSET = 32768


def _dsa_kernel(qi_ref, ki_ref, wi_ref, qa_ref, ka_ref, va_ref, za0_ref, za1_ref, o_ref,
                key_ref, hi_ref, lo_ref, wb_ref, thr_ref, m_ref, l_ref, acc_ref, *, top_k, tq, kc):
    i = pl.program_id(1)
    nch = ((i + 1) * tq) // kc
    t_idx = i * tq + lax.broadcasted_iota(jnp.int32, (tq, kc), 0)
    lane = lax.broadcasted_iota(jnp.int32, (tq, kc), 1)
    rep = kc // LANES

    wi = wi_ref[...]
    for h in range(IDX_HEADS):
        wb_ref[h] = jnp.broadcast_to(wi[:, h:h + 1], (tq, kc))

    def score_body(c, carry):
        k0 = pl.multiple_of(c * kc, kc)
        kch = ki_ref[0, pl.ds(k0, kc), :]
        acc = None
        for g in range(IDX_HEADS // IDX_HEAD_GROUP):
            qg = qi_ref[0, g * IDX_HEAD_GROUP:(g + 1) * IDX_HEAD_GROUP].reshape(IDX_HEAD_GROUP * tq, HEAD_DIM)
            rel = _nt_dot(qg, kch)
            for hh in range(IDX_HEAD_GROUP):
                term = wb_ref[g * IDX_HEAD_GROUP + hh] * jnp.maximum(rel[hh * tq:(hh + 1) * tq], 0.0)
                acc = term if acc is None else acc + term
        score = jnp.where(k0 + lane <= t_idx, acc, NEG_INF)
        key = _sortable_key(score)
        key_ref[:, pl.ds(k0, kc)] = key
        key_t = key.T
        hi_ref[pl.ds(k0, kc), :] = lax.shift_right_arithmetic(key_t, 16).astype(jnp.int16)
        lo_ref[pl.ds(k0, kc), :] = ((key_t & 0xFFFF) - HALF_OFFSET).astype(jnp.int16)
        return carry

    lax.fori_loop(0, nch, score_body, 0)

    kcw = ATT_MULT * kc
    nchw = (nch + ATT_MULT - 1) // ATT_MULT
    for r in range(1, ATT_MULT):
        @pl.when(nch % ATT_MULT == r)
        def _(r=r):
            for e in range(ATT_MULT - r):
                sl = pl.ds(pl.multiple_of((nch + e) * kc, kc), kc)
                key_ref[:, sl] = _sortable_key(jnp.full((tq, kc), NEG_INF, F32))
                hi_ref[sl, :] = jnp.full((kc, tq), -HALF_OFFSET, jnp.int16)
                lo_ref[sl, :] = jnp.full((kc, tq), -HALF_OFFSET, jnp.int16)

    need_bisect = (i + 1) * tq > top_k

    @pl.when(jnp.logical_not(need_bisect))
    def _():
        thr_ref[...] = jnp.full(thr_ref.shape, INT_MIN, jnp.int32)

    @pl.when(need_bisect)
    def _():
        zero = jnp.zeros((PACK_ROWS, tq), jnp.int16)

        def count(ref, pred):
            def body(c, cnt):
                v = ref[pl.ds(pl.multiple_of(c * kcw, kcw), kcw), :]
                ind = jnp.where(pred(v), jnp.int16(1), jnp.int16(0))
                for r in range(kcw // PACK_ROWS):
                    cnt = cnt + ind[r * PACK_ROWS:(r + 1) * PACK_ROWS]
                return cnt
            cnt = lax.fori_loop(0, nchw, body, zero)
            return jnp.sum(cnt.astype(F32), axis=0, keepdims=True)

        def bisect16(ref, need):
            def bit_body(b, u):
                ub = u | lax.shift_left(jnp.int32(1), 15 - b)
                cand = jnp.broadcast_to((ub - HALF_OFFSET).astype(jnp.int16), (kcw, tq))
                tot = count(ref, lambda v: v >= cand)
                return jnp.where(tot >= need, ub, u)
            return lax.fori_loop(0, 16, bit_body, jnp.zeros((1, tq), jnp.int32))

        u_hi = bisect16(hi_ref, float(top_k))
        t_hi = jnp.broadcast_to((u_hi - HALF_OFFSET).astype(jnp.int16), (kcw, tq))
        above = count(hi_ref, lambda v: v > t_hi)

        def mask_body(c, carry):
            sl = pl.ds(pl.multiple_of(c * kcw, kcw), kcw)
            lo_ref[sl, :] = jnp.where(hi_ref[sl, :] == t_hi, lo_ref[sl, :], jnp.int16(-HALF_OFFSET))
            return carry

        lax.fori_loop(0, nchw, mask_body, 0)
        u_lo = bisect16(lo_ref, float(top_k) - above)
        thr_q = (u_hi - HALF_OFFSET) * (2 * HALF_OFFSET) + u_lo
        t_q = i * tq + lax.broadcasted_iota(jnp.int32, (1, tq), 1)
        thr_q = jnp.where(t_q < top_k, jnp.int32(INT_MIN), thr_q)
        for r in range(tq // LANES):
            blk = jnp.broadcast_to(thr_q[:, r * LANES:(r + 1) * LANES], (LANES, LANES))
            thr_ref[r * LANES:(r + 1) * LANES, :] = blk.T

    kca = kcw
    rep_a = kca // LANES
    t_idx_a = i * tq + lax.broadcasted_iota(jnp.int32, (tq, kca), 0)
    lane_a = lax.broadcasted_iota(jnp.int32, (tq, kca), 1)
    thr_k = jnp.concatenate([thr_ref[...]] * rep_a, axis=1)
    group = A_HEADS // A_KV_HEADS
    m_ref[...] = jnp.full(m_ref.shape, NEG_INF, F32)
    l_ref[...] = jnp.zeros(l_ref.shape, F32)
    acc_ref[...] = jnp.zeros(acc_ref.shape, F32)

    def att_body(c, carry):
        k0 = pl.multiple_of(c * kca, kca)
        kk = key_ref[:, pl.ds(k0, kca)]
        bias = jnp.where(k0 + lane_a <= t_idx_a, jnp.where(kk >= thr_k, 0.0, MASK_BIAS), MASK_BIAS)
        bias = jnp.concatenate([bias] * group, axis=0)
        for c2 in range(A_KV_HEADS):
            q = qa_ref[0, c2 * group:(c2 + 1) * group].reshape(group * tq, HEAD_DIM)
            s = _nt_dot(q, ka_ref[0, c2, pl.ds(k0, kca), :]) + bias
            m_prev = m_ref[c2]
            m_next = jnp.maximum(m_prev, jnp.max(s, axis=1, keepdims=True))
            p = jnp.exp2(s - jnp.concatenate([m_next] * rep_a, axis=1))
            alpha = jnp.exp2(m_prev - m_next)
            l_ref[c2] = alpha * l_ref[c2] + jnp.sum(p, axis=1, keepdims=True)
            m_ref[c2] = m_next
            acc_ref[c2] = alpha * acc_ref[c2] + jnp.dot(
                p.astype(MXU_DTYPE), va_ref[0, c2, pl.ds(k0, kca), :], preferred_element_type=F32)
        return carry

    lax.fori_loop(0, nchw, att_body, 0)
    heads_per_half = HALF_TILE // HEAD_DIM
    for c2 in range(A_KV_HEADS):
        for g in range(group):
            h = c2 * group + g
            rows = slice(g * tq, (g + 1) * tq)
            out = acc_ref[c2, rows, :] / l_ref[c2, rows, :]
            z_ref = (za0_ref, za1_ref)[h // heads_per_half]
            za = z_ref[:, (h % heads_per_half) * HEAD_DIM:(h % heads_per_half + 1) * HEAD_DIM]
            o_ref[:, h * HEAD_DIM:(h + 1) * HEAD_DIM] = (out * _silu(za)).astype(o_ref.dtype)


def _dsa(qi, ki, wi, qa, ka, va, proj, batch, seq, top_k, tq, kc):
    nqt = seq // tq
    n = batch * seq
    row = lambda b, i: b * nqt + i
    group = A_HEADS // A_KV_HEADS
    kern = functools.partial(_dsa_kernel, top_k=top_k, tq=tq, kc=kc)
    stat = pltpu.VMEM((A_KV_HEADS, group * tq, HEAD_DIM), F32)
    return pl.pallas_call(
        kern,
        grid=(batch, nqt),
        in_specs=[
            pl.BlockSpec((1, IDX_HEADS, tq, HEAD_DIM), lambda b, i: (b, 0, i, 0)),
            pl.BlockSpec((1, seq, HEAD_DIM), lambda b, i: (b, 0, 0)),
            pl.BlockSpec((tq, LANES), lambda b, i: (row(b, i), 0)),
            pl.BlockSpec((1, A_HEADS, tq, HEAD_DIM), lambda b, i: (b, 0, i, 0)),
            pl.BlockSpec((1, A_KV_HEADS, seq, HEAD_DIM), lambda b, i: (b, 0, 0, 0)),
            pl.BlockSpec((1, A_KV_HEADS, seq, HEAD_DIM), lambda b, i: (b, 0, 0, 0)),
            pl.BlockSpec((tq, HALF_TILE), lambda b, i: (row(b, i), COL_ZA // HALF_TILE)),
            pl.BlockSpec((tq, HALF_TILE), lambda b, i: (row(b, i), COL_ZA // HALF_TILE + 1)),
        ],
        out_specs=pl.BlockSpec((tq, A_WIDTH), lambda b, i: (row(b, i), 0)),
        out_shape=jax.ShapeDtypeStruct((n, A_WIDTH), MXU_DTYPE),
        scratch_shapes=[
            pltpu.VMEM((tq, seq), jnp.int32),
            pltpu.VMEM((seq, tq), jnp.int16),
            pltpu.VMEM((seq, tq), jnp.int16),
            pltpu.VMEM((IDX_HEADS, tq, kc), F32),
            pltpu.VMEM((tq, LANES), jnp.int32),
            stat, stat, stat,
        ],
        compiler_params=pltpu.CompilerParams(
            dimension_semantics=("arbitrary", "arbitrary"), vmem_limit_bytes=VMEM_LIMIT),
        name="dsa",
    )(qi, ki, wi, qa, ka, va, proj, proj)


SB_SUB = 128
SB_DEAD_LOG2 = -160.0


def _sb_chunk(q, kch, vch, tri2, rests, row0, key0, kc):
    nsub = q.shape[0] // SB_SUB
    z = _nt_dot(q, kch)
    his, los, lbs, sums, masks = [], [], [], [], []
    for s in range(nsub):
        zs = z[s * SB_SUB:(s + 1) * SB_SUB]
        m0 = jnp.minimum(zs, 0.0)
        t1 = m0 - zs
        sp = jnp.log2(1.0 + jnp.exp2(m0 + t1))
        log_beta = m0 - sp
        log_keep = t1 - sp
        mask = None
        first_row = row0 + s * SB_SUB
        if key0 is not None and key0 + kc - 1 >= first_row:
            mask = (key0 + lax.broadcasted_iota(jnp.int32, (SB_SUB, kc), 1)
                    < first_row + lax.broadcasted_iota(jnp.int32, (SB_SUB, kc), 0))
            log_keep = jnp.where(mask, log_keep, 0.0)
        hi = log_keep.astype(MXU_DTYPE)
        his.append(hi)
        los.append((log_keep - hi.astype(F32)).astype(MXU_DTYPE))
        lbs.append(log_beta)
        masks.append(mask)
        sums.append(jnp.sum(log_keep, axis=1, keepdims=True))
    hl = jnp.concatenate([jnp.concatenate(his, axis=0), jnp.concatenate(los, axis=0)], axis=1)
    after = jnp.dot(hl, tri2, preferred_element_type=F32)
    es = []
    for s in range(nsub):
        e = jnp.exp2(lbs[s] + after[s * SB_SUB:(s + 1) * SB_SUB] + rests[s])
        if masks[s] is not None:
            e = jnp.where(masks[s], e, 0.0)
        es.append(e.astype(MXU_DTYPE))
    pv = jnp.dot(jnp.concatenate(es, axis=0), vch, preferred_element_type=F32)
    return [r + sm for r, sm in zip(rests, sums)], pv


def _sb_kernel(q_ref, k_ref, v_ref, z_ref, o_ref, *, tq, kc):
    i = pl.program_id(2)
    nsub = tq // SB_SUB
    ndiag = tq // kc
    tri = jnp.where(lax.broadcasted_iota(jnp.int32, (kc, kc), 0) > lax.broadcasted_iota(jnp.int32, (kc, kc), 1),
                    1.0, 0.0).astype(MXU_DTYPE)
    tri2 = jnp.concatenate([tri, tri], axis=0)
    q = q_ref[0, 0]

    def kv(k0):
        k0 = pl.multiple_of(k0, kc)
        return k_ref[0, 0, pl.ds(k0, kc), :], v_ref[0, 0, pl.ds(k0, kc), :]

    rests = [jnp.zeros((SB_SUB, 1), F32) for _ in range(nsub)]
    acc = jnp.zeros((tq, HEAD_DIM), F32)
    for d in reversed(range(ndiag)):
        row0 = d * kc
        s0 = row0 // SB_SUB
        kch, vch = kv(i * tq + d * kc)
        new, pv = _sb_chunk(q[row0:], kch, vch, tri2, rests[s0:], row0, d * kc, kc)
        rests = rests[:s0] + new
        acc = jnp.concatenate([acc[:row0], acc[row0:] + pv], axis=0) if row0 else acc + pv

    nfull = i * ndiag

    def alive(rests):
        top = rests[0]
        for r in rests[1:]:
            top = jnp.maximum(top, r)
        return (jnp.max(top) > SB_DEAD_LOG2).astype(jnp.int32)

    def cond(carry):
        return jnp.logical_and(carry[0] < nfull, carry[1] > 0)

    def body(carry):
        j, _, rests, acc = carry
        kch, vch = kv((nfull - 1 - j) * kc)
        rests, pv = _sb_chunk(q, kch, vch, tri2, list(rests), 0, None, kc)
        return j + 1, alive(rests), tuple(rests), acc + pv

    acc = lax.while_loop(cond, body, (jnp.int32(0), alive(rests), tuple(rests), acc))[3]
    o_ref[...] = (acc * _silu(z_ref[...])).astype(o_ref.dtype)


def _sb(qc, kc_, vc, proj, batch, seq, tq, kc):
    nqb = seq // tq
    n = batch * seq
    col_zc = COL_ZC // HEAD_DIM
    kern = functools.partial(_sb_kernel, tq=tq, kc=kc)
    return pl.pallas_call(
        kern,
        grid=(batch, C_HEADS, nqb),
        in_specs=[
            pl.BlockSpec((1, 1, tq, HEAD_DIM), lambda b, h, i: (b, h, i, 0)),
            pl.BlockSpec((1, 1, seq, HEAD_DIM), lambda b, h, i: (b, h, 0, 0)),
            pl.BlockSpec((1, 1, seq, HEAD_DIM), lambda b, h, i: (b, h, 0, 0)),
            pl.BlockSpec((tq, HEAD_DIM), lambda b, h, i: (b * nqb + i, col_zc + h)),
        ],
        out_specs=pl.BlockSpec((tq, HEAD_DIM), lambda b, h, i: (b * nqb + i, h)),
        out_shape=jax.ShapeDtypeStruct((n, C_WIDTH), MXU_DTYPE),
        compiler_params=pltpu.CompilerParams(
            dimension_semantics=("arbitrary", "arbitrary", "arbitrary"), vmem_limit_bytes=VMEM_LIMIT),
        name="sb",
    )(qc, kc_, vc, proj)


CONV_HIST = 32


def _conv_kernel(ua_ref, ug_ref, zb_ref, cw_ref, cb_ref, lw_ref, lb_ref, o_ref, ysh, *, ts, width, rc):
    ch = o_ref.shape[1]

    @pl.when(pl.program_id(1) == 0)
    def _():
        ysh[0, 0:CONV_HIST, :] = jnp.zeros((CONV_HIST, ch), F32)

    @pl.when(pl.program_id(1) > 0)
    def _():
        ysh[0, 0:CONV_HIST, :] = ysh[0, ts:ts + CONV_HIST, :]

    ysh[0, CONV_HIST:CONV_HIST + ts, :] = ua_ref[...] * jax.nn.sigmoid(ug_ref[...])
    span = CONV_HIST + ts - SUBLANES
    for r in range(1, SUBLANES):
        ysh[r, 0:span, :] = ysh[0, r:r + span, :]

    def chunk(c, carry):
        r0 = pl.multiple_of(c * rc, rc)
        acc = jnp.broadcast_to(cb_ref[...], (rc, ch))
        for j in range(width):
            off = CONV_HIST - (width - 1) + j
            r = off % SUBLANES
            acc = acc + cw_ref[j:j + 1, :] * ysh[r, pl.ds(r0 + (off - r), rc), :]
        mu = jnp.mean(acc, axis=1, keepdims=True)
        d = acc - mu
        var = jnp.mean(d * d, axis=1, keepdims=True)
        y = d * lax.rsqrt(var + EPS) * lw_ref[...] + lb_ref[...]
        o_ref[pl.ds(r0, rc), :] = (_silu(y) * _silu(zb_ref[pl.ds(r0, rc), :])).astype(o_ref.dtype)
        return carry

    lax.fori_loop(0, ts // rc, chunk, 0)


def _conv(proj, conv_w, conv_b, ln_w, ln_b, batch, seq, ts, rc):
    n = batch * seq
    width, ch = conv_w.shape
    nst = seq // ts
    vec = pl.BlockSpec((1, ch), lambda b, s: (0, 0))
    kern = functools.partial(_conv_kernel, ts=ts, width=width, rc=rc)
    return pl.pallas_call(
        kern,
        grid=(batch, nst),
        in_specs=[
            pl.BlockSpec((ts, ch), lambda b, s: (b * nst + s, COL_UB // ch)),
            pl.BlockSpec((ts, ch), lambda b, s: (b * nst + s, COL_UB // ch + 1)),
            pl.BlockSpec((ts, ch), lambda b, s: (b * nst + s, COL_ZB // ch)),
            pl.BlockSpec((width, ch), lambda b, s: (0, 0)),
            vec, vec, vec,
        ],
        out_specs=pl.BlockSpec((ts, ch), lambda b, s: (b * nst + s, 0)),
        out_shape=jax.ShapeDtypeStruct((n, ch), MXU_DTYPE),
        scratch_shapes=[pltpu.VMEM((SUBLANES, CONV_HIST + ts, ch), F32)],
        compiler_params=pltpu.CompilerParams(
            dimension_semantics=("arbitrary", "arbitrary"), vmem_limit_bytes=VMEM_LIMIT),
        name="conv",
    )(proj, proj, proj, conv_w, conv_b.reshape(1, ch), ln_w.reshape(1, ch), ln_b.reshape(1, ch))


def _merge_kernel(ya_ref, yb_ref, yc_ref, g0_ref, g1_ref, g2_ref, wa_ref, wb_ref, wc_ref, o_ref):
    ya = jnp.dot(ya_ref[...], wa_ref[...], preferred_element_type=F32)
    yb = jnp.dot(yb_ref[...], wb_ref[...], preferred_element_type=F32)
    yc = jnp.dot(yc_ref[...], wc_ref[...], preferred_element_type=F32)
    merged = (jax.nn.sigmoid(g0_ref[...]) * ya + jax.nn.sigmoid(g1_ref[...]) * yb
              + jax.nn.sigmoid(g2_ref[...]) * yc)
    o_ref[...] = merged.astype(o_ref.dtype)


def _merge(ya, yb, yc, proj, wa, wb, wc, col_gates, tm):
    n = ya.shape[0]
    d = wa.shape[1]
    act = lambda w: pl.BlockSpec((tm, w), lambda i: (i, 0))
    gate = lambda k: pl.BlockSpec((tm, d), lambda i, k=k: (i, col_gates // d + k))
    wspec = lambda w: pl.BlockSpec(w.shape, lambda i: (0, 0))
    return pl.pallas_call(
        _merge_kernel,
        grid=(n // tm,),
        in_specs=[act(ya.shape[1]), act(yb.shape[1]), act(yc.shape[1]), gate(0), gate(1), gate(2),
                  wspec(wa), wspec(wb), wspec(wc)],
        out_specs=pl.BlockSpec((tm, d), lambda i: (i, 0)),
        out_shape=jax.ShapeDtypeStruct((n, d), MXU_DTYPE),
        compiler_params=pltpu.CompilerParams(
            dimension_semantics=("arbitrary",), vmem_limit_bytes=VMEM_LIMIT),
        name="merge",
    )(ya, yb, yc, proj, proj, proj, wa, wb, wc)


def _outproj_kernel(m_ref, x_ref, w_ref, fw_ref, o_ref, *, final_norm):
    y = x_ref[...] + jnp.dot(m_ref[...], w_ref[...], preferred_element_type=F32)
    if final_norm:
        ms = jnp.mean(y * y, axis=-1, keepdims=True)
        y = y * lax.rsqrt(ms + EPS) * fw_ref[...]
    o_ref[...] = y


def _outproj(merged, x2d, w_out, final_w, final_norm, tm):
    n, d = x2d.shape
    kern = functools.partial(_outproj_kernel, final_norm=final_norm)
    return pl.pallas_call(
        kern,
        grid=(n // tm,),
        in_specs=[
            pl.BlockSpec((tm, d), lambda i: (i, 0)),
            pl.BlockSpec((tm, d), lambda i: (i, 0)),
            pl.BlockSpec((d, d), lambda i: (0, 0)),
            pl.BlockSpec((1, d), lambda i: (0, 0)),
        ],
        out_specs=pl.BlockSpec((tm, d), lambda i: (i, 0)),
        out_shape=jax.ShapeDtypeStruct((n, d), F32),
        compiler_params=pltpu.CompilerParams(
            dimension_semantics=("arbitrary",), vmem_limit_bytes=VMEM_LIMIT),
        name="outproj",
    )(merged, x2d, w_out, final_w.reshape(1, d))


def _wprep_kernel(a_ref, b_ref, o_ref, *, used):
    j = pl.program_id(1)
    nfirst = COL_UB // COL_TILE
    col = j * COL_TILE + lax.broadcasted_iota(jnp.int32, o_ref.shape, 1)

    @pl.when(j < nfirst)
    def _():
        o_ref[...] = jnp.where(col < FIRST_RUN, a_ref[...].T, 0.0).astype(o_ref.dtype)

    @pl.when(j >= nfirst)
    def _():
        shift = FIRST_RUN - (COL_UB - COL_TILE)
        win = jnp.concatenate([a_ref[...], b_ref[...]], axis=0)[shift:shift + COL_TILE]
        o_ref[...] = jnp.where(col < used, win.T, 0.0).astype(o_ref.dtype)


def _prepare_w_in(w_in, d, conv_ch, kb):
    assert conv_ch == CONV_CH and w_in.shape[-1] == FIRST_RUN + (COL_GATES - COL_UB) + N_BRANCH * d
    assert (FIRST_RUN - COL_UB) % SUBLANES == 0
    depth = w_in.shape[0]
    used = COL_GATES + N_BRANCH * d
    np_ = -(-used // COL_TILE) * COL_TILE
    nfirst = COL_UB // COL_TILE
    w_t = jnp.swapaxes(w_in, 1, 2)
    src = lambda back: pl.BlockSpec(
        (None, COL_TILE, kb), lambda l, j, r: (l, jnp.where(j < nfirst, j, j - back), r))
    return pl.pallas_call(
        functools.partial(_wprep_kernel, used=used),
        grid=(depth, np_ // COL_TILE, d // kb),
        in_specs=[src(1), src(0)],
        out_specs=pl.BlockSpec((None, kb, COL_TILE), lambda l, j, r: (l, r, j)),
        out_shape=jax.ShapeDtypeStruct((depth, d, np_), MXU_DTYPE),
        compiler_params=pltpu.CompilerParams(
            dimension_semantics=("arbitrary", "arbitrary", "arbitrary"), vmem_limit_bytes=VMEM_LIMIT),
        name="wprep",
    )(w_t, w_t)


def _rope_tables(seq):
    inv = 1.0 / (ROPE_THETA ** (jnp.arange(0, HEAD_DIM, 2, dtype=F32) / HEAD_DIM))
    ang = jnp.arange(seq, dtype=F32)[:, None] * inv[None, :]
    cos, sin = jnp.cos(ang), jnp.sin(ang)
    return jnp.concatenate([cos, cos], axis=1), jnp.concatenate([-sin, sin], axis=1)


def kernel(x, norm_w, w_in, conv_w, conv_b, conv_ln_w, conv_ln_b, w_a_out, w_b_out, w_c_out, w_out, final_norm_w):
    batch, seq, d = x.shape
    depth = norm_w.shape[0]
    conv_ch = conv_b.shape[-1]
    n = batch * seq
    assert seq % BLOCK_Q == 0 and conv_ch == COL_TILE and conv_w.shape[1] - 1 <= CONV_HIST
    top_k = min(IDX_TOPK, seq // 4)
    kc = 256
    assert seq % kc == 0 and COL_GATES % d == 0

    w_prep = _prepare_w_in(w_in, d, conv_ch, min(512, d))
    wa, wb, wc, wo = (w.astype(MXU_DTYPE) for w in (w_a_out, w_b_out, w_c_out, w_out))
    cw = conv_w.reshape(depth, conv_w.shape[1], conv_ch)
    cos_t, sin_t = _rope_tables(seq)

    tm_in = min(1024, n)
    ts_prep = min(256, seq)
    ts_conv = min(512, seq)
    tq_dsa = min(256, seq)
    tq_sb = min(512, seq)
    assert tq_dsa % kc == 0 and tq_sb % kc == 0 and seq % tq_sb == 0 and seq % (ATT_MULT * kc) == 0

    x2d = x.reshape(n, d)
    for layer in range(depth):
        proj = _inproj(x2d, norm_w[layer], w_prep, layer, tm_in, COL_TILE)
        qi, qa, qc, kcs, vcs, ka, va, ki, wi = _prep(proj, cos_t, sin_t, batch, seq, ts_prep)
        ya = _dsa(qi, ki, wi, qa, ka, va, proj, batch, seq, top_k, tq_dsa, kc)
        yc = _sb(qc, kcs, vcs, proj, batch, seq, tq_sb, kc)
        yb = _conv(proj, cw[layer], conv_b[layer], conv_ln_w[layer], conv_ln_b[layer], batch, seq, ts_conv, 64)
        merged = _merge(ya, yb, yc, proj, wa[layer], wb[layer], wc[layer], COL_GATES, min(256, n))
        x2d = _outproj(merged, x2d, wo[layer], final_norm_w, layer == depth - 1, min(512, n))
    return x2d.reshape(batch, seq, d)
```

```python
import functools

import jax
import jax.numpy as jnp
from jax import lax
from jax.experimental import pallas as pl
from jax.experimental.pallas import tpu as pltpu

HEAD_DIM = 128
ROPE_THETA = 10000.0
BLOCK_Q = 128
EPS = 1e-6
NEG_INF = -1e30
MASK_BIAS = -2e30
A_HEADS = 8
A_KV_HEADS = 2
IDX_HEADS = 16
IDX_TOPK = 256
C_HEADS = 8
N_BRANCH = 3
LANES = 128
SUBLANES = 8
LOG2E = 1.4426950408889634
INT_MIN = -(2 ** 31)

MXU_DTYPE = jnp.bfloat16
F32 = jnp.float32

A_WIDTH = A_HEADS * HEAD_DIM
A_KV_WIDTH = A_KV_HEADS * HEAD_DIM
IDX_WIDTH = IDX_HEADS * HEAD_DIM
C_WIDTH = C_HEADS * HEAD_DIM

COL_TILE = 1024
CONV_CH = COL_TILE
COL_QA = 0
COL_KA = COL_QA + A_WIDTH
COL_VA = COL_KA + A_KV_WIDTH
COL_ZA = COL_VA + A_KV_WIDTH
COL_QI = COL_ZA + A_WIDTH
COL_KI = COL_QI + IDX_WIDTH
COL_WI = COL_KI + HEAD_DIM
FIRST_RUN = COL_WI + IDX_HEADS
COL_UB = -(-FIRST_RUN // COL_TILE) * COL_TILE
COL_ZB = COL_UB + 2 * CONV_CH
COL_QC = COL_ZB + CONV_CH
COL_KC = COL_QC + C_WIDTH
COL_VC = COL_KC + C_WIDTH
COL_ZC = COL_VC + C_WIDTH
COL_GATES = COL_ZC + C_WIDTH
HALF_TILE = COL_TILE // 2

VMEM_LIMIT = 56 * 1024 * 1024


def _nt_dot(a, b):
    return lax.dot_general(a, b, (((1,), (1,)), ((), ())), preferred_element_type=F32)


def _silu(x):
    return x * jax.nn.sigmoid(x)


def _inproj_kernel(x_ref, nw_ref, w_ref, o_ref, h_ref):
    @pl.when(pl.program_id(1) == 0)
    def _():
        xf = x_ref[...]
        ms = jnp.mean(xf * xf, axis=-1, keepdims=True)
        h_ref[...] = (xf * lax.rsqrt(ms + EPS) * nw_ref[...]).astype(h_ref.dtype)

    o_ref[...] = jnp.dot(h_ref[...], w_ref[...], preferred_element_type=F32)


def _inproj(x2d, norm_w, w_prep, layer, tm, tn):
    n, d = x2d.shape
    np_ = w_prep.shape[2]
    return pl.pallas_call(
        _inproj_kernel,
        grid=(n // tm, np_ // tn),
        in_specs=[
            pl.BlockSpec((tm, d), lambda i, j: (i, 0)),
            pl.BlockSpec((1, d), lambda i, j: (0, 0)),
            pl.BlockSpec((None, d, tn), lambda i, j: (layer, 0, j)),
        ],
        out_specs=pl.BlockSpec((tm, tn), lambda i, j: (i, j)),
        out_shape=jax.ShapeDtypeStruct((n, np_), F32),
        scratch_shapes=[pltpu.VMEM((tm, d), MXU_DTYPE)],
        compiler_params=pltpu.CompilerParams(
            dimension_semantics=("arbitrary", "arbitrary"), vmem_limit_bytes=VMEM_LIMIT),
        name="inproj",
    )(x2d, norm_w.reshape(1, d), w_prep)


def _rope(x, cos, sin_signed):
    return x * cos + pltpu.roll(x, HEAD_DIM // 2, axis=1) * sin_signed


def _prep_kernel(qa_ref, kva_ref, qi0_ref, qi1_ref, qi2_ref, qi3_ref, ki_ref, wi_ref, qc_ref, kc_ref, vc_ref,
                 cos_ref, sin_ref,
                 qi_o, qa_o, qc_o, kc_o, vc_o, ka_o, va_o, ki_o, wi_o, *, qk_scale, wi_scale):
    cos = cos_ref[...]
    sin = sin_ref[...]
    heads_per_half = HALF_TILE // HEAD_DIM
    for t, src in enumerate((qi0_ref, qi1_ref, qi2_ref, qi3_ref)):
        for h in range(heads_per_half):
            x = src[:, h * HEAD_DIM:(h + 1) * HEAD_DIM]
            qi_o[0, t * heads_per_half + h] = _rope(x, cos, sin).astype(qi_o.dtype)
    for h in range(A_HEADS):
        x = qa_ref[:, h * HEAD_DIM:(h + 1) * HEAD_DIM]
        qa_o[0, h] = (_rope(x, cos, sin) * qk_scale).astype(qa_o.dtype)
    for h in range(C_HEADS):
        sl = slice(h * HEAD_DIM, (h + 1) * HEAD_DIM)
        qc_o[0, h] = (qc_ref[:, sl] * qk_scale).astype(qc_o.dtype)
        kc_o[0, h] = kc_ref[:, sl].astype(kc_o.dtype)
        vc_o[0, h] = vc_ref[:, sl].astype(vc_o.dtype)
    for h in range(A_KV_HEADS):
        ka_o[0, h] = _rope(kva_ref[:, h * HEAD_DIM:(h + 1) * HEAD_DIM], cos, sin).astype(ka_o.dtype)
        va_o[0, h] = kva_ref[:, A_KV_WIDTH + h * HEAD_DIM:A_KV_WIDTH + (h + 1) * HEAD_DIM].astype(va_o.dtype)
    ki_o[0] = _rope(ki_ref[...], cos, sin).astype(ki_o.dtype)
    wi_o[...] = wi_ref[...] * wi_scale[0] * wi_scale[1]


def _prep(proj, cos_t, sin_t, batch, seq, ts):
    n = proj.shape[0]
    nst = seq // ts
    row = lambda b, s: b * nst + s
    col = lambda off, w: pl.BlockSpec((ts, w), lambda b, s: (row(b, s), off // w))
    tab = pl.BlockSpec((ts, HEAD_DIM), lambda b, s: (s, 0))
    in_cols = ([col(COL_QA, A_WIDTH), col(COL_KA, 2 * A_KV_WIDTH)]
               + [col(COL_QI + t * HALF_TILE, HALF_TILE) for t in range(IDX_WIDTH // HALF_TILE)]
               + [col(COL_KI, HEAD_DIM), col(COL_WI, LANES),
                  col(COL_QC, C_WIDTH), col(COL_KC, C_WIDTH), col(COL_VC, C_WIDTH)])
    heads = lambda nh: pl.BlockSpec((1, nh, ts, HEAD_DIM), lambda b, s: (b, 0, s, 0))
    hshape = lambda nh: jax.ShapeDtypeStruct((batch, nh, seq, HEAD_DIM), MXU_DTYPE)
    kern = functools.partial(
        _prep_kernel, qk_scale=HEAD_DIM ** -0.5 * LOG2E, wi_scale=(IDX_HEADS ** -0.5, HEAD_DIM ** -0.5))
    return pl.pallas_call(
        kern,
        grid=(batch, nst),
        in_specs=in_cols + [tab, tab],
        out_specs=[
            heads(IDX_HEADS), heads(A_HEADS), heads(C_HEADS), heads(C_HEADS), heads(C_HEADS),
            heads(A_KV_HEADS), heads(A_KV_HEADS),
            pl.BlockSpec((1, ts, HEAD_DIM), lambda b, s: (b, s, 0)),
            pl.BlockSpec((ts, LANES), lambda b, s: (row(b, s), 0)),
        ],
        out_shape=[
            hshape(IDX_HEADS), hshape(A_HEADS), hshape(C_HEADS), hshape(C_HEADS), hshape(C_HEADS),
            hshape(A_KV_HEADS), hshape(A_KV_HEADS),
            jax.ShapeDtypeStruct((batch, seq, HEAD_DIM), MXU_DTYPE),
            jax.ShapeDtypeStruct((n, LANES), F32),
        ],
        compiler_params=pltpu.CompilerParams(
            dimension_semantics=("arbitrary", "arbitrary"), vmem_limit_bytes=VMEM_LIMIT),
        name="prep",
    )(*([proj] * len(in_cols)), cos_t, sin_t)


def _sortable_key(x):
    bits = lax.bitcast_convert_type(x, jnp.int32)
    return jnp.where(bits < 0, bits ^ jnp.int32(0x7FFFFFFF), bits)


IDX_HEAD_GROUP = 1
ATT_MULT = 2


PACK_ROWS = 16
HALF_OFFSET = 32768


def _dsa_kernel(qi_ref, ki_ref, wi_ref, qa_ref, ka_ref, va_ref, za0_ref, za1_ref, o_ref,
                key_ref, hi_ref, lo_ref, wb_ref, thr_ref, ties_ref, seen_ref, flag_ref, m_ref, l_ref, acc_ref,
                *, top_k, tq, kc):
    i = pl.program_id(1)
    nch = ((i + 1) * tq) // kc
    t_idx = i * tq + lax.broadcasted_iota(jnp.int32, (tq, kc), 0)
    lane = lax.broadcasted_iota(jnp.int32, (tq, kc), 1)
    rep = kc // LANES

    wi = wi_ref[...]
    for h in range(IDX_HEADS):
        wb_ref[h] = jnp.broadcast_to(wi[:, h:h + 1], (tq, kc))

    def score_body(c, carry):
        k0 = pl.multiple_of(c * kc, kc)
        kch = ki_ref[0, pl.ds(k0, kc), :]
        acc = None
        for g in range(IDX_HEADS // IDX_HEAD_GROUP):
            qg = qi_ref[0, g * IDX_HEAD_GROUP:(g + 1) * IDX_HEAD_GROUP].reshape(IDX_HEAD_GROUP * tq, HEAD_DIM)
            rel = _nt_dot(qg, kch)
            for hh in range(IDX_HEAD_GROUP):
                term = wb_ref[g * IDX_HEAD_GROUP + hh] * jnp.maximum(rel[hh * tq:(hh + 1) * tq], 0.0)
                acc = term if acc is None else acc + term
        score = jnp.where(k0 + lane <= t_idx, acc, NEG_INF)
        key = _sortable_key(score)
        key_ref[:, pl.ds(k0, kc)] = key
        key_t = key.T
        hi_ref[pl.ds(k0, kc), :] = lax.shift_right_arithmetic(key_t, 16).astype(jnp.int16)
        lo_ref[pl.ds(k0, kc), :] = ((key_t & 0xFFFF) - HALF_OFFSET).astype(jnp.int16)
        return carry

    lax.fori_loop(0, nch, score_body, 0)

    kcw = ATT_MULT * kc
    nchw = (nch + ATT_MULT - 1) // ATT_MULT
    for r in range(1, ATT_MULT):
        @pl.when(nch % ATT_MULT == r)
        def _(r=r):
            for e in range(ATT_MULT - r):
                sl = pl.ds(pl.multiple_of((nch + e) * kc, kc), kc)
                key_ref[:, sl] = _sortable_key(jnp.full((tq, kc), NEG_INF, F32))
                hi_ref[sl, :] = jnp.full((kc, tq), -HALF_OFFSET, jnp.int16)
                lo_ref[sl, :] = jnp.full((kc, tq), -HALF_OFFSET, jnp.int16)

    need_bisect = (i + 1) * tq > top_k

    @pl.when(jnp.logical_not(need_bisect))
    def _():
        thr_ref[...] = jnp.full(thr_ref.shape, INT_MIN, jnp.int32)
        ties_ref[...] = jnp.zeros(ties_ref.shape, F32)
        flag_ref[...] = jnp.zeros(flag_ref.shape, F32)

    @pl.when(need_bisect)
    def _():
        zero = jnp.zeros((PACK_ROWS, tq), jnp.int16)

        def count(ref, pred):
            def body(c, cnt):
                v = ref[pl.ds(pl.multiple_of(c * kcw, kcw), kcw), :]
                ind = jnp.where(pred(v), jnp.int16(1), jnp.int16(0))
                for r in range(kcw // PACK_ROWS):
                    cnt = cnt + ind[r * PACK_ROWS:(r + 1) * PACK_ROWS]
                return cnt
            cnt = lax.fori_loop(0, nchw, body, zero)
            return jnp.sum(cnt.astype(F32), axis=0, keepdims=True)

        def bisect16(ref, need):
            def bit_body(b, u):
                ub = u | lax.shift_left(jnp.int32(1), 15 - b)
                cand = jnp.broadcast_to((ub - HALF_OFFSET).astype(jnp.int16), (kcw, tq))
                tot = count(ref, lambda v: v >= cand)
                return jnp.where(tot >= need, ub, u)
            return lax.fori_loop(0, 16, bit_body, jnp.zeros((1, tq), jnp.int32))

        u_hi = bisect16(hi_ref, float(top_k))
        t_hi = jnp.broadcast_to((u_hi - HALF_OFFSET).astype(jnp.int16), (kcw, tq))
        above = count(hi_ref, lambda v: v > t_hi)

        def mask_body(c, carry):
            sl = pl.ds(pl.multiple_of(c * kcw, kcw), kcw)
            lo_ref[sl, :] = jnp.where(hi_ref[sl, :] == t_hi, lo_ref[sl, :], jnp.int16(-HALF_OFFSET))
            return carry

        lax.fori_loop(0, nchw, mask_body, 0)
        u_lo = bisect16(lo_ref, float(top_k) - above)
        t_lo = jnp.broadcast_to((u_lo - HALF_OFFSET).astype(jnp.int16), (kcw, tq))
        ties_kept = float(top_k) - above - count(lo_ref, lambda v: v > t_lo)
        surplus = jnp.max(count(lo_ref, lambda v: v == t_lo) - ties_kept, axis=1, keepdims=True)
        flag_ref[...] = jnp.broadcast_to(surplus, flag_ref.shape)
        thr_q = (u_hi - HALF_OFFSET) * (2 * HALF_OFFSET) + u_lo
        t_q = i * tq + lax.broadcasted_iota(jnp.int32, (1, tq), 1)
        thr_q = jnp.where(t_q < top_k, jnp.int32(INT_MIN), thr_q)
        for r in range(tq // LANES):
            sl = slice(r * LANES, (r + 1) * LANES)
            thr_ref[sl, :] = jnp.broadcast_to(thr_q[:, sl], (LANES, LANES)).T
            ties_ref[sl, :] = jnp.broadcast_to(ties_kept[:, sl], (LANES, LANES)).T

    kca = kcw
    rep_a = kca // LANES
    t_idx_a = i * tq + lax.broadcasted_iota(jnp.int32, (tq, kca), 0)
    lane_a = lax.broadcasted_iota(jnp.int32, (tq, kca), 1)
    thr_k = jnp.concatenate([thr_ref[...]] * rep_a, axis=1)
    group = A_HEADS // A_KV_HEADS
    m_ref[...] = jnp.full(m_ref.shape, NEG_INF, F32)
    l_ref[...] = jnp.zeros(l_ref.shape, F32)
    acc_ref[...] = jnp.zeros(acc_ref.shape, F32)
    ties_k = jnp.concatenate([ties_ref[...]] * rep_a, axis=1)
    upto = jnp.where(lax.broadcasted_iota(jnp.int32, (kca, kca), 0) <= lax.broadcasted_iota(jnp.int32, (kca, kca), 1),
                     1.0, 0.0).astype(MXU_DTYPE)
    seen_ref[...] = jnp.zeros(seen_ref.shape, F32)

    surplus_ties = flag_ref[0, 0] > 0.0

    def pick_with_ties(kk):
        tie = jnp.where(kk == thr_k, 1.0, 0.0)
        seen = seen_ref[...]
        rank = jnp.dot(tie.astype(MXU_DTYPE), upto, preferred_element_type=F32) + jnp.concatenate([seen] * rep_a, axis=1)
        seen_ref[...] = seen + jnp.sum(tie, axis=1, keepdims=True)
        keep_tie = jnp.where(rank <= ties_k, tie, 0.0)
        return jnp.where(kk > thr_k, 0.0, jnp.where(keep_tie > 0.0, 0.0, MASK_BIAS))

    def pick_all_ties(kk):
        return jnp.where(kk >= thr_k, 0.0, MASK_BIAS)

    def att_body(c, carry, pick):
        k0 = pl.multiple_of(c * kca, kca)
        kk = key_ref[:, pl.ds(k0, kca)]
        bias = jnp.where(k0 + lane_a <= t_idx_a, pick(kk), MASK_BIAS)
        bias = jnp.concatenate([bias] * group, axis=0)
        for c2 in range(A_KV_HEADS):
            q = qa_ref[0, c2 * group:(c2 + 1) * group].reshape(group * tq, HEAD_DIM)
            s = _nt_dot(q, ka_ref[0, c2, pl.ds(k0, kca), :]) + bias
            m_prev = m_ref[c2]
            m_next = jnp.maximum(m_prev, jnp.max(s, axis=1, keepdims=True))
            p = jnp.exp2(s - jnp.concatenate([m_next] * rep_a, axis=1))
            alpha = jnp.exp2(m_prev - m_next)
            l_ref[c2] = alpha * l_ref[c2] + jnp.sum(p, axis=1, keepdims=True)
            m_ref[c2] = m_next
            acc_ref[c2] = alpha * acc_ref[c2] + jnp.dot(
                p.astype(MXU_DTYPE), va_ref[0, c2, pl.ds(k0, kca), :], preferred_element_type=F32)
        return carry

    def walk(pick):
        return lax.fori_loop(0, nchw, functools.partial(att_body, pick=pick), 0)

    lax.cond(surplus_ties, lambda: walk(pick_with_ties), lambda: walk(pick_all_ties))
    heads_per_half = HALF_TILE // HEAD_DIM
    for c2 in range(A_KV_HEADS):
        for g in range(group):
            h = c2 * group + g
            rows = slice(g * tq, (g + 1) * tq)
            out = acc_ref[c2, rows, :] / l_ref[c2, rows, :]
            z_ref = (za0_ref, za1_ref)[h // heads_per_half]
            za = z_ref[:, (h % heads_per_half) * HEAD_DIM:(h % heads_per_half + 1) * HEAD_DIM]
            o_ref[:, h * HEAD_DIM:(h + 1) * HEAD_DIM] = (out * _silu(za)).astype(o_ref.dtype)


def _dsa(qi, ki, wi, qa, ka, va, proj, batch, seq, top_k, tq, kc):
    nqt = seq // tq
    n = batch * seq
    row = lambda b, i: b * nqt + i
    group = A_HEADS // A_KV_HEADS
    kern = functools.partial(_dsa_kernel, top_k=top_k, tq=tq, kc=kc)
    stat = pltpu.VMEM((A_KV_HEADS, group * tq, HEAD_DIM), F32)
    return pl.pallas_call(
        kern,
        grid=(batch, nqt),
        in_specs=[
            pl.BlockSpec((1, IDX_HEADS, tq, HEAD_DIM), lambda b, i: (b, 0, i, 0)),
            pl.BlockSpec((1, seq, HEAD_DIM), lambda b, i: (b, 0, 0)),
            pl.BlockSpec((tq, LANES), lambda b, i: (row(b, i), 0)),
            pl.BlockSpec((1, A_HEADS, tq, HEAD_DIM), lambda b, i: (b, 0, i, 0)),
            pl.BlockSpec((1, A_KV_HEADS, seq, HEAD_DIM), lambda b, i: (b, 0, 0, 0)),
            pl.BlockSpec((1, A_KV_HEADS, seq, HEAD_DIM), lambda b, i: (b, 0, 0, 0)),
            pl.BlockSpec((tq, HALF_TILE), lambda b, i: (row(b, i), COL_ZA // HALF_TILE)),
            pl.BlockSpec((tq, HALF_TILE), lambda b, i: (row(b, i), COL_ZA // HALF_TILE + 1)),
        ],
        out_specs=pl.BlockSpec((tq, A_WIDTH), lambda b, i: (row(b, i), 0)),
        out_shape=jax.ShapeDtypeStruct((n, A_WIDTH), MXU_DTYPE),
        scratch_shapes=[
            pltpu.VMEM((tq, seq), jnp.int32),
            pltpu.VMEM((seq, tq), jnp.int16),
            pltpu.VMEM((seq, tq), jnp.int16),
            pltpu.VMEM((IDX_HEADS, tq, kc), F32),
            pltpu.VMEM((tq, LANES), jnp.int32),
            pltpu.VMEM((tq, LANES), F32),
            pltpu.VMEM((tq, LANES), F32),
            pltpu.VMEM((SUBLANES, LANES), F32),
            stat, stat, stat,
        ],
        compiler_params=pltpu.CompilerParams(
            dimension_semantics=("arbitrary", "arbitrary"), vmem_limit_bytes=VMEM_LIMIT),
        name="dsa",
    )(qi, ki, wi, qa, ka, va, proj, proj)


SB_SUB = 128
SB_DEAD_LOG2 = -160.0


def _sb_chunk(q, kch, vch, tri2, rests, row0, key0, kc):
    nsub = q.shape[0] // SB_SUB
    z = _nt_dot(q, kch)
    his, los, lbs, sums, masks = [], [], [], [], []
    for s in range(nsub):
        zs = z[s * SB_SUB:(s + 1) * SB_SUB]
        m0 = jnp.minimum(zs, 0.0)
        t1 = m0 - zs
        sp = jnp.log2(1.0 + jnp.exp2(m0 + t1))
        log_beta = m0 - sp
        log_keep = t1 - sp
        mask = None
        first_row = row0 + s * SB_SUB
        if key0 is not None and key0 + kc - 1 >= first_row:
            mask = (key0 + lax.broadcasted_iota(jnp.int32, (SB_SUB, kc), 1)
                    < first_row + lax.broadcasted_iota(jnp.int32, (SB_SUB, kc), 0))
            log_keep = jnp.where(mask, log_keep, 0.0)
        hi = log_keep.astype(MXU_DTYPE)
        his.append(hi)
        los.append((log_keep - hi.astype(F32)).astype(MXU_DTYPE))
        lbs.append(log_beta)
        masks.append(mask)
        sums.append(jnp.sum(log_keep, axis=1, keepdims=True))
    hl = jnp.concatenate([jnp.concatenate(his, axis=0), jnp.concatenate(los, axis=0)], axis=1)
    after = jnp.dot(hl, tri2, preferred_element_type=F32)
    es = []
    for s in range(nsub):
        e = jnp.exp2(lbs[s] + after[s * SB_SUB:(s + 1) * SB_SUB] + rests[s])
        if masks[s] is not None:
            e = jnp.where(masks[s], e, 0.0)
        es.append(e.astype(MXU_DTYPE))
    pv = jnp.dot(jnp.concatenate(es, axis=0), vch, preferred_element_type=F32)
    return [r + sm for r, sm in zip(rests, sums)], pv


def _sb_kernel(q_ref, k_ref, v_ref, z_ref, o_ref, *, tq, kc):
    i = pl.program_id(2)
    nsub = tq // SB_SUB
    ndiag = tq // kc
    tri = jnp.where(lax.broadcasted_iota(jnp.int32, (kc, kc), 0) > lax.broadcasted_iota(jnp.int32, (kc, kc), 1),
                    1.0, 0.0).astype(MXU_DTYPE)
    tri2 = jnp.concatenate([tri, tri], axis=0)
    q = q_ref[0, 0]

    def kv(k0):
        k0 = pl.multiple_of(k0, kc)
        return k_ref[0, 0, pl.ds(k0, kc), :], v_ref[0, 0, pl.ds(k0, kc), :]

    rests = [jnp.zeros((SB_SUB, 1), F32) for _ in range(nsub)]
    acc = jnp.zeros((tq, HEAD_DIM), F32)
    for d in reversed(range(ndiag)):
        row0 = d * kc
        s0 = row0 // SB_SUB
        kch, vch = kv(i * tq + d * kc)
        new, pv = _sb_chunk(q[row0:], kch, vch, tri2, rests[s0:], row0, d * kc, kc)
        rests = rests[:s0] + new
        acc = jnp.concatenate([acc[:row0], acc[row0:] + pv], axis=0) if row0 else acc + pv

    nfull = i * ndiag

    def alive(rests):
        top = rests[0]
        for r in rests[1:]:
            top = jnp.maximum(top, r)
        return (jnp.max(top) > SB_DEAD_LOG2).astype(jnp.int32)

    def cond(carry):
        return jnp.logical_and(carry[0] < nfull, carry[1] > 0)

    def body(carry):
        j, _, rests, acc = carry
        kch, vch = kv((nfull - 1 - j) * kc)
        rests, pv = _sb_chunk(q, kch, vch, tri2, list(rests), 0, None, kc)
        return j + 1, alive(rests), tuple(rests), acc + pv

    acc = lax.while_loop(cond, body, (jnp.int32(0), alive(rests), tuple(rests), acc))[3]
    o_ref[...] = (acc * _silu(z_ref[...])).astype(o_ref.dtype)


def _sb(qc, kc_, vc, proj, batch, seq, tq, kc):
    nqb = seq // tq
    n = batch * seq
    col_zc = COL_ZC // HEAD_DIM
    kern = functools.partial(_sb_kernel, tq=tq, kc=kc)
    return pl.pallas_call(
        kern,
        grid=(batch, C_HEADS, nqb),
        in_specs=[
            pl.BlockSpec((1, 1, tq, HEAD_DIM), lambda b, h, i: (b, h, i, 0)),
            pl.BlockSpec((1, 1, seq, HEAD_DIM), lambda b, h, i: (b, h, 0, 0)),
            pl.BlockSpec((1, 1, seq, HEAD_DIM), lambda b, h, i: (b, h, 0, 0)),
            pl.BlockSpec((tq, HEAD_DIM), lambda b, h, i: (b * nqb + i, col_zc + h)),
        ],
        out_specs=pl.BlockSpec((tq, HEAD_DIM), lambda b, h, i: (b * nqb + i, h)),
        out_shape=jax.ShapeDtypeStruct((n, C_WIDTH), MXU_DTYPE),
        compiler_params=pltpu.CompilerParams(
            dimension_semantics=("arbitrary", "arbitrary", "arbitrary"), vmem_limit_bytes=VMEM_LIMIT),
        name="sb",
    )(qc, kc_, vc, proj)


CONV_HIST = 32


def _conv_kernel(ua_ref, ug_ref, zb_ref, cw_ref, cb_ref, lw_ref, lb_ref, o_ref, ysh, *, ts, width, rc):
    ch = o_ref.shape[1]

    @pl.when(pl.program_id(1) == 0)
    def _():
        ysh[0, 0:CONV_HIST, :] = jnp.zeros((CONV_HIST, ch), F32)

    @pl.when(pl.program_id(1) > 0)
    def _():
        ysh[0, 0:CONV_HIST, :] = ysh[0, ts:ts + CONV_HIST, :]

    ysh[0, CONV_HIST:CONV_HIST + ts, :] = ua_ref[...] * jax.nn.sigmoid(ug_ref[...])
    span = CONV_HIST + ts - SUBLANES
    for r in range(1, SUBLANES):
        ysh[r, 0:span, :] = ysh[0, r:r + span, :]

    def chunk(c, carry):
        r0 = pl.multiple_of(c * rc, rc)
        acc = jnp.broadcast_to(cb_ref[...], (rc, ch))
        for j in range(width):
            off = CONV_HIST - (width - 1) + j
            r = off % SUBLANES
            acc = acc + cw_ref[j:j + 1, :] * ysh[r, pl.ds(r0 + (off - r), rc), :]
        mu = jnp.mean(acc, axis=1, keepdims=True)
        d = acc - mu
        var = jnp.mean(d * d, axis=1, keepdims=True)
        y = d * lax.rsqrt(var + EPS) * lw_ref[...] + lb_ref[...]
        o_ref[pl.ds(r0, rc), :] = (_silu(y) * _silu(zb_ref[pl.ds(r0, rc), :])).astype(o_ref.dtype)
        return carry

    lax.fori_loop(0, ts // rc, chunk, 0)


def _conv(proj, conv_w, conv_b, ln_w, ln_b, batch, seq, ts, rc):
    n = batch * seq
    width, ch = conv_w.shape
    nst = seq // ts
    vec = pl.BlockSpec((1, ch), lambda b, s: (0, 0))
    kern = functools.partial(_conv_kernel, ts=ts, width=width, rc=rc)
    return pl.pallas_call(
        kern,
        grid=(batch, nst),
        in_specs=[
            pl.BlockSpec((ts, ch), lambda b, s: (b * nst + s, COL_UB // ch)),
            pl.BlockSpec((ts, ch), lambda b, s: (b * nst + s, COL_UB // ch + 1)),
            pl.BlockSpec((ts, ch), lambda b, s: (b * nst + s, COL_ZB // ch)),
            pl.BlockSpec((width, ch), lambda b, s: (0, 0)),
            vec, vec, vec,
        ],
        out_specs=pl.BlockSpec((ts, ch), lambda b, s: (b * nst + s, 0)),
        out_shape=jax.ShapeDtypeStruct((n, ch), MXU_DTYPE),
        scratch_shapes=[pltpu.VMEM((SUBLANES, CONV_HIST + ts, ch), F32)],
        compiler_params=pltpu.CompilerParams(
            dimension_semantics=("arbitrary", "arbitrary"), vmem_limit_bytes=VMEM_LIMIT),
        name="conv",
    )(proj, proj, proj, conv_w, conv_b.reshape(1, ch), ln_w.reshape(1, ch), ln_b.reshape(1, ch))


def _merge_kernel(ya_ref, yb_ref, yc_ref, g0_ref, g1_ref, g2_ref, wa_ref, wb_ref, wc_ref, o_ref):
    ya = jnp.dot(ya_ref[...], wa_ref[...], preferred_element_type=F32)
    yb = jnp.dot(yb_ref[...], wb_ref[...], preferred_element_type=F32)
    yc = jnp.dot(yc_ref[...], wc_ref[...], preferred_element_type=F32)
    merged = (jax.nn.sigmoid(g0_ref[...]) * ya + jax.nn.sigmoid(g1_ref[...]) * yb
              + jax.nn.sigmoid(g2_ref[...]) * yc)
    o_ref[...] = merged.astype(o_ref.dtype)


def _merge(ya, yb, yc, proj, wa, wb, wc, col_gates, tm):
    n = ya.shape[0]
    d = wa.shape[1]
    act = lambda w: pl.BlockSpec((tm, w), lambda i: (i, 0))
    gate = lambda k: pl.BlockSpec((tm, d), lambda i, k=k: (i, col_gates // d + k))
    wspec = lambda w: pl.BlockSpec(w.shape, lambda i: (0, 0))
    return pl.pallas_call(
        _merge_kernel,
        grid=(n // tm,),
        in_specs=[act(ya.shape[1]), act(yb.shape[1]), act(yc.shape[1]), gate(0), gate(1), gate(2),
                  wspec(wa), wspec(wb), wspec(wc)],
        out_specs=pl.BlockSpec((tm, d), lambda i: (i, 0)),
        out_shape=jax.ShapeDtypeStruct((n, d), MXU_DTYPE),
        compiler_params=pltpu.CompilerParams(
            dimension_semantics=("arbitrary",), vmem_limit_bytes=VMEM_LIMIT),
        name="merge",
    )(ya, yb, yc, proj, proj, proj, wa, wb, wc)


def _outproj_kernel(m_ref, x_ref, w_ref, fw_ref, o_ref, *, final_norm):
    y = x_ref[...] + jnp.dot(m_ref[...], w_ref[...], preferred_element_type=F32)
    if final_norm:
        ms = jnp.mean(y * y, axis=-1, keepdims=True)
        y = y * lax.rsqrt(ms + EPS) * fw_ref[...]
    o_ref[...] = y


def _outproj(merged, x2d, w_out, final_w, final_norm, tm):
    n, d = x2d.shape
    kern = functools.partial(_outproj_kernel, final_norm=final_norm)
    return pl.pallas_call(
        kern,
        grid=(n // tm,),
        in_specs=[
            pl.BlockSpec((tm, d), lambda i: (i, 0)),
            pl.BlockSpec((tm, d), lambda i: (i, 0)),
            pl.BlockSpec((d, d), lambda i: (0, 0)),
            pl.BlockSpec((1, d), lambda i: (0, 0)),
        ],
        out_specs=pl.BlockSpec((tm, d), lambda i: (i, 0)),
        out_shape=jax.ShapeDtypeStruct((n, d), F32),
        compiler_params=pltpu.CompilerParams(
            dimension_semantics=("arbitrary",), vmem_limit_bytes=VMEM_LIMIT),
        name="outproj",
    )(merged, x2d, w_out, final_w.reshape(1, d))


def _wprep_kernel(a_ref, b_ref, o_ref, *, used):
    j = pl.program_id(1)
    nfirst = COL_UB // COL_TILE
    col = j * COL_TILE + lax.broadcasted_iota(jnp.int32, o_ref.shape, 1)

    @pl.when(j < nfirst)
    def _():
        o_ref[...] = jnp.where(col < FIRST_RUN, a_ref[...].T, 0.0).astype(o_ref.dtype)

    @pl.when(j >= nfirst)
    def _():
        shift = FIRST_RUN - (COL_UB - COL_TILE)
        win = jnp.concatenate([a_ref[...], b_ref[...]], axis=0)[shift:shift + COL_TILE]
        o_ref[...] = jnp.where(col < used, win.T, 0.0).astype(o_ref.dtype)


def _prepare_w_in(w_in, d, conv_ch, kb):
    assert conv_ch == CONV_CH and w_in.shape[-1] == FIRST_RUN + (COL_GATES - COL_UB) + N_BRANCH * d
    assert (FIRST_RUN - COL_UB) % SUBLANES == 0
    depth = w_in.shape[0]
    used = COL_GATES + N_BRANCH * d
    np_ = -(-used // COL_TILE) * COL_TILE
    nfirst = COL_UB // COL_TILE
    w_t = jnp.swapaxes(w_in, 1, 2)
    src = lambda back: pl.BlockSpec(
        (None, COL_TILE, kb), lambda l, j, r: (l, jnp.where(j < nfirst, j, j - back), r))
    return pl.pallas_call(
        functools.partial(_wprep_kernel, used=used),
        grid=(depth, np_ // COL_TILE, d // kb),
        in_specs=[src(1), src(0)],
        out_specs=pl.BlockSpec((None, kb, COL_TILE), lambda l, j, r: (l, r, j)),
        out_shape=jax.ShapeDtypeStruct((depth, d, np_), MXU_DTYPE),
        compiler_params=pltpu.CompilerParams(
            dimension_semantics=("arbitrary", "arbitrary", "arbitrary"), vmem_limit_bytes=VMEM_LIMIT),
        name="wprep",
    )(w_t, w_t)


def _rope_tables(seq):
    inv = 1.0 / (ROPE_THETA ** (jnp.arange(0, HEAD_DIM, 2, dtype=F32) / HEAD_DIM))
    ang = jnp.arange(seq, dtype=F32)[:, None] * inv[None, :]
    cos, sin = jnp.cos(ang), jnp.sin(ang)
    return jnp.concatenate([cos, cos], axis=1), jnp.concatenate([-sin, sin], axis=1)


def kernel(x, norm_w, w_in, conv_w, conv_b, conv_ln_w, conv_ln_b, w_a_out, w_b_out, w_c_out, w_out, final_norm_w):
    batch, seq, d = x.shape
    depth = norm_w.shape[0]
    conv_ch = conv_b.shape[-1]
    n = batch * seq
    assert seq % BLOCK_Q == 0 and conv_ch == COL_TILE and conv_w.shape[1] - 1 <= CONV_HIST
    top_k = min(IDX_TOPK, seq // 4)
    kc = 256
    assert seq % kc == 0 and COL_GATES % d == 0

    w_prep = _prepare_w_in(w_in, d, conv_ch, min(512, d))
    wa, wb, wc, wo = (w.astype(MXU_DTYPE) for w in (w_a_out, w_b_out, w_c_out, w_out))
    cw = conv_w.reshape(depth, conv_w.shape[1], conv_ch)
    cos_t, sin_t = _rope_tables(seq)

    tm_in = min(1024, n)
    ts_prep = min(256, seq)
    ts_conv = min(512, seq)
    tq_dsa = min(256, seq)
    tq_sb = min(512, seq)
    assert tq_dsa % kc == 0 and tq_sb % kc == 0 and seq % tq_sb == 0 and seq % (ATT_MULT * kc) == 0

    x2d = x.reshape(n, d)
    for layer in range(depth):
        proj = _inproj(x2d, norm_w[layer], w_prep, layer, tm_in, COL_TILE)
        qi, qa, qc, kcs, vcs, ka, va, ki, wi = _prep(proj, cos_t, sin_t, batch, seq, ts_prep)
        ya = _dsa(qi, ki, wi, qa, ka, va, proj, batch, seq, top_k, tq_dsa, kc)
        yc = _sb(qc, kcs, vcs, proj, batch, seq, tq_sb, kc)
        yb = _conv(proj, cw[layer], conv_b[layer], conv_ln_w[layer], conv_ln_b[layer], batch, seq, ts_conv, 64)
        merged = _merge(ya, yb, yc, proj, wa[layer], wb[layer], wc[layer], COL_GATES, min(256, n))
        x2d = _outproj(merged, x2d, wo[layer], final_norm_w, layer == depth - 1, min(512, n))
    return x2d.reshape(batch, seq, d)
```

```python
import functools

import jax
import jax.numpy as jnp
from jax import lax
from jax.experimental import pallas as pl
from jax.experimental.pallas import tpu as pltpu

HEAD_DIM = 128
ROPE_THETA = 10000.0
BLOCK_Q = 128
EPS = 1e-6
NEG_INF = -1e30
MASK_BIAS = -2e30
A_HEADS = 8
A_KV_HEADS = 2
IDX_HEADS = 16
IDX_TOPK = 256
C_HEADS = 8
N_BRANCH = 3
LANES = 128
SUBLANES = 8
LOG2E = 1.4426950408889634
INT_MIN = -(2 ** 31)

MXU_DTYPE = jnp.bfloat16
F32 = jnp.float32

A_WIDTH = A_HEADS * HEAD_DIM
A_KV_WIDTH = A_KV_HEADS * HEAD_DIM
IDX_WIDTH = IDX_HEADS * HEAD_DIM
C_WIDTH = C_HEADS * HEAD_DIM

COL_TILE = 1024
CONV_CH = COL_TILE
COL_QA = 0
COL_KA = COL_QA + A_WIDTH
COL_VA = COL_KA + A_KV_WIDTH
COL_ZA = COL_VA + A_KV_WIDTH
COL_QI = COL_ZA + A_WIDTH
COL_KI = COL_QI + IDX_WIDTH
COL_WI = COL_KI + HEAD_DIM
FIRST_RUN = COL_WI + IDX_HEADS
COL_UB = -(-FIRST_RUN // COL_TILE) * COL_TILE
COL_ZB = COL_UB + 2 * CONV_CH
COL_QC = COL_ZB + CONV_CH
COL_KC = COL_QC + C_WIDTH
COL_VC = COL_KC + C_WIDTH
COL_ZC = COL_VC + C_WIDTH
COL_GATES = COL_ZC + C_WIDTH
HALF_TILE = COL_TILE // 2

VMEM_LIMIT = 56 * 1024 * 1024


def _nt_dot(a, b):
    return lax.dot_general(a, b, (((1,), (1,)), ((), ())), preferred_element_type=F32)


def _silu(x):
    return x * jax.nn.sigmoid(x)


def _inproj_kernel(x_ref, nw_ref, w_ref, o_ref, h_ref):
    @pl.when(pl.program_id(1) == 0)
    def _():
        xf = x_ref[...]
        ms = jnp.mean(xf * xf, axis=-1, keepdims=True)
        h_ref[...] = (xf * lax.rsqrt(ms + EPS) * nw_ref[...]).astype(h_ref.dtype)

    o_ref[...] = jnp.dot(h_ref[...], w_ref[...], preferred_element_type=F32)


def _inproj(x2d, norm_w, w_prep, layer, tm, tn):
    n, d = x2d.shape
    np_ = w_prep.shape[2]
    return pl.pallas_call(
        _inproj_kernel,
        grid=(n // tm, np_ // tn),
        in_specs=[
            pl.BlockSpec((tm, d), lambda i, j: (i, 0)),
            pl.BlockSpec((1, d), lambda i, j: (0, 0)),
            pl.BlockSpec((None, d, tn), lambda i, j: (layer, 0, j)),
        ],
        out_specs=pl.BlockSpec((tm, tn), lambda i, j: (i, j)),
        out_shape=jax.ShapeDtypeStruct((n, np_), F32),
        scratch_shapes=[pltpu.VMEM((tm, d), MXU_DTYPE)],
        compiler_params=pltpu.CompilerParams(
            dimension_semantics=("arbitrary", "arbitrary"), vmem_limit_bytes=VMEM_LIMIT),
        name="inproj",
    )(x2d, norm_w.reshape(1, d), w_prep)


def _rope(x, cos, sin_signed):
    return x * cos + pltpu.roll(x, HEAD_DIM // 2, axis=1) * sin_signed


def _prep_kernel(qa_ref, kva_ref, qi0_ref, qi1_ref, qi2_ref, qi3_ref, ki_ref, wi_ref, qc_ref, kc_ref, vc_ref,
                 cos_ref, sin_ref,
                 qi_o, qa_o, qc_o, kc_o, vc_o, ka_o, va_o, ki_o, wi_o, *, qk_scale, wi_scale):
    cos = cos_ref[...]
    sin = sin_ref[...]
    heads_per_half = HALF_TILE // HEAD_DIM
    for t, src in enumerate((qi0_ref, qi1_ref, qi2_ref, qi3_ref)):
        for h in range(heads_per_half):
            x = src[:, h * HEAD_DIM:(h + 1) * HEAD_DIM]
            qi_o[0, t * heads_per_half + h] = _rope(x, cos, sin).astype(qi_o.dtype)
    for h in range(A_HEADS):
        x = qa_ref[:, h * HEAD_DIM:(h + 1) * HEAD_DIM]
        qa_o[0, h] = (_rope(x, cos, sin) * qk_scale).astype(qa_o.dtype)
    for h in range(C_HEADS):
        sl = slice(h * HEAD_DIM, (h + 1) * HEAD_DIM)
        qc_o[0, h] = (qc_ref[:, sl] * qk_scale).astype(qc_o.dtype)
        kc_o[0, h] = kc_ref[:, sl].astype(kc_o.dtype)
        vc_o[0, h] = vc_ref[:, sl].astype(vc_o.dtype)
    for h in range(A_KV_HEADS):
        ka_o[0, h] = _rope(kva_ref[:, h * HEAD_DIM:(h + 1) * HEAD_DIM], cos, sin).astype(ka_o.dtype)
        va_o[0, h] = kva_ref[:, A_KV_WIDTH + h * HEAD_DIM:A_KV_WIDTH + (h + 1) * HEAD_DIM].astype(va_o.dtype)
    ki_o[0] = _rope(ki_ref[...], cos, sin).astype(ki_o.dtype)
    wi_o[...] = wi_ref[...] * wi_scale[0] * wi_scale[1]


def _prep(proj, cos_t, sin_t, batch, seq, ts):
    n = proj.shape[0]
    nst = seq // ts
    row = lambda b, s: b * nst + s
    col = lambda off, w: pl.BlockSpec((ts, w), lambda b, s: (row(b, s), off // w))
    tab = pl.BlockSpec((ts, HEAD_DIM), lambda b, s: (s, 0))
    in_cols = ([col(COL_QA, A_WIDTH), col(COL_KA, 2 * A_KV_WIDTH)]
               + [col(COL_QI + t * HALF_TILE, HALF_TILE) for t in range(IDX_WIDTH // HALF_TILE)]
               + [col(COL_KI, HEAD_DIM), col(COL_WI, LANES),
                  col(COL_QC, C_WIDTH), col(COL_KC, C_WIDTH), col(COL_VC, C_WIDTH)])
    heads = lambda nh: pl.BlockSpec((1, nh, ts, HEAD_DIM), lambda b, s: (b, 0, s, 0))
    hshape = lambda nh: jax.ShapeDtypeStruct((batch, nh, seq, HEAD_DIM), MXU_DTYPE)
    kern = functools.partial(
        _prep_kernel, qk_scale=HEAD_DIM ** -0.5 * LOG2E, wi_scale=(IDX_HEADS ** -0.5, HEAD_DIM ** -0.5))
    return pl.pallas_call(
        kern,
        grid=(batch, nst),
        in_specs=in_cols + [tab, tab],
        out_specs=[
            heads(IDX_HEADS), heads(A_HEADS), heads(C_HEADS), heads(C_HEADS), heads(C_HEADS),
            heads(A_KV_HEADS), heads(A_KV_HEADS),
            pl.BlockSpec((1, ts, HEAD_DIM), lambda b, s: (b, s, 0)),
            pl.BlockSpec((ts, LANES), lambda b, s: (row(b, s), 0)),
        ],
        out_shape=[
            hshape(IDX_HEADS), hshape(A_HEADS), hshape(C_HEADS), hshape(C_HEADS), hshape(C_HEADS),
            hshape(A_KV_HEADS), hshape(A_KV_HEADS),
            jax.ShapeDtypeStruct((batch, seq, HEAD_DIM), MXU_DTYPE),
            jax.ShapeDtypeStruct((n, LANES), F32),
        ],
        compiler_params=pltpu.CompilerParams(
            dimension_semantics=("arbitrary", "arbitrary"), vmem_limit_bytes=VMEM_LIMIT),
        name="prep",
    )(*([proj] * len(in_cols)), cos_t, sin_t)


def _sortable_key(x):
    bits = lax.bitcast_convert_type(x, jnp.int32)
    return jnp.where(bits < 0, bits ^ jnp.int32(0x7FFFFFFF), bits)


IDX_HEAD_GROUP = 1
ATT_MULT = 2


PACK_ROWS = 16
HALF_OFFSET = 32768


def _dsa_kernel(qi_ref, ki_ref, wi_ref, qa_ref, ka_ref, va_ref, za0_ref, za1_ref, o_ref,
                key_ref, hi_ref, lo_ref, wb_ref, thr_ref, ties_ref, seen_ref, flag_ref, m_ref, l_ref, acc_ref,
                *, top_k, tq, kc):
    i = pl.program_id(1)
    nch = ((i + 1) * tq) // kc
    t_idx = i * tq + lax.broadcasted_iota(jnp.int32, (tq, kc), 0)
    lane = lax.broadcasted_iota(jnp.int32, (tq, kc), 1)
    rep = kc // LANES

    wi = wi_ref[...]
    for h in range(IDX_HEADS):
        wb_ref[h] = jnp.broadcast_to(wi[:, h:h + 1], (tq, kc))

    def score_body(c, carry):
        k0 = pl.multiple_of(c * kc, kc)
        kch = ki_ref[0, pl.ds(k0, kc), :]
        acc = None
        for g in range(IDX_HEADS // IDX_HEAD_GROUP):
            qg = qi_ref[0, g * IDX_HEAD_GROUP:(g + 1) * IDX_HEAD_GROUP].reshape(IDX_HEAD_GROUP * tq, HEAD_DIM)
            rel = _nt_dot(qg, kch)
            for hh in range(IDX_HEAD_GROUP):
                term = wb_ref[g * IDX_HEAD_GROUP + hh] * jnp.maximum(rel[hh * tq:(hh + 1) * tq], 0.0)
                acc = term if acc is None else acc + term
        score = jnp.where(k0 + lane <= t_idx, acc, NEG_INF)
        key = _sortable_key(score)
        key_ref[:, pl.ds(k0, kc)] = key
        key_t = key.T
        hi_ref[pl.ds(k0, kc), :] = lax.shift_right_arithmetic(key_t, 16).astype(jnp.int16)
        lo_ref[pl.ds(k0, kc), :] = ((key_t & 0xFFFF) - HALF_OFFSET).astype(jnp.int16)
        return carry

    lax.fori_loop(0, nch, score_body, 0)

    kcw = ATT_MULT * kc
    nchw = (nch + ATT_MULT - 1) // ATT_MULT
    for r in range(1, ATT_MULT):
        @pl.when(nch % ATT_MULT == r)
        def _(r=r):
            for e in range(ATT_MULT - r):
                sl = pl.ds(pl.multiple_of((nch + e) * kc, kc), kc)
                key_ref[:, sl] = _sortable_key(jnp.full((tq, kc), NEG_INF, F32))
                hi_ref[sl, :] = jnp.full((kc, tq), -HALF_OFFSET, jnp.int16)
                lo_ref[sl, :] = jnp.full((kc, tq), -HALF_OFFSET, jnp.int16)

    need_bisect = (i + 1) * tq > top_k

    @pl.when(jnp.logical_not(need_bisect))
    def _():
        thr_ref[...] = jnp.full(thr_ref.shape, INT_MIN, jnp.int32)
        ties_ref[...] = jnp.zeros(ties_ref.shape, F32)
        flag_ref[...] = jnp.zeros(flag_ref.shape, F32)

    @pl.when(need_bisect)
    def _():
        zero = jnp.zeros((PACK_ROWS, tq), jnp.int16)

        def count(ref, pred):
            def body(c, cnt):
                v = ref[pl.ds(pl.multiple_of(c * kcw, kcw), kcw), :]
                ind = jnp.where(pred(v), jnp.int16(1), jnp.int16(0))
                for r in range(kcw // PACK_ROWS):
                    cnt = cnt + ind[r * PACK_ROWS:(r + 1) * PACK_ROWS]
                return cnt
            cnt = lax.fori_loop(0, nchw, body, zero)
            return jnp.sum(cnt.astype(F32), axis=0, keepdims=True)

        def bisect16(ref, need):
            def bit_body(b, u):
                ub = u | lax.shift_left(jnp.int32(1), 15 - b)
                cand = jnp.broadcast_to((ub - HALF_OFFSET).astype(jnp.int16), (kcw, tq))
                tot = count(ref, lambda v: v >= cand)
                return jnp.where(tot >= need, ub, u)
            return lax.fori_loop(0, 16, bit_body, jnp.zeros((1, tq), jnp.int32))

        u_hi = bisect16(hi_ref, float(top_k))
        t_hi = jnp.broadcast_to((u_hi - HALF_OFFSET).astype(jnp.int16), (kcw, tq))
        above = count(hi_ref, lambda v: v > t_hi)

        def mask_body(c, carry):
            sl = pl.ds(pl.multiple_of(c * kcw, kcw), kcw)
            lo_ref[sl, :] = jnp.where(hi_ref[sl, :] == t_hi, lo_ref[sl, :], jnp.int16(-HALF_OFFSET))
            return carry

        lax.fori_loop(0, nchw, mask_body, 0)
        u_lo = bisect16(lo_ref, float(top_k) - above)
        t_lo = jnp.broadcast_to((u_lo - HALF_OFFSET).astype(jnp.int16), (kcw, tq))
        ties_kept = float(top_k) - above - count(lo_ref, lambda v: v > t_lo)
        surplus = jnp.max(count(lo_ref, lambda v: v == t_lo) - ties_kept, axis=1, keepdims=True)
        flag_ref[...] = jnp.broadcast_to(surplus, flag_ref.shape)
        thr_q = (u_hi - HALF_OFFSET) * (2 * HALF_OFFSET) + u_lo
        t_q = i * tq + lax.broadcasted_iota(jnp.int32, (1, tq), 1)
        thr_q = jnp.where(t_q < top_k, jnp.int32(INT_MIN), thr_q)
        for r in range(tq // LANES):
            sl = slice(r * LANES, (r + 1) * LANES)
            thr_ref[sl, :] = jnp.broadcast_to(thr_q[:, sl], (LANES, LANES)).T
            ties_ref[sl, :] = jnp.broadcast_to(ties_kept[:, sl], (LANES, LANES)).T

    kca = kcw
    rep_a = kca // LANES
    t_idx_a = i * tq + lax.broadcasted_iota(jnp.int32, (tq, kca), 0)
    lane_a = lax.broadcasted_iota(jnp.int32, (tq, kca), 1)
    thr_k = jnp.concatenate([thr_ref[...]] * rep_a, axis=1)
    group = A_HEADS // A_KV_HEADS
    m_ref[...] = jnp.full(m_ref.shape, NEG_INF, F32)
    l_ref[...] = jnp.zeros(l_ref.shape, F32)
    acc_ref[...] = jnp.zeros(acc_ref.shape, F32)
    ties_k = jnp.concatenate([ties_ref[...]] * rep_a, axis=1)
    upto = jnp.where(lax.broadcasted_iota(jnp.int32, (kca, kca), 0) <= lax.broadcasted_iota(jnp.int32, (kca, kca), 1),
                     1.0, 0.0).astype(MXU_DTYPE)
    seen_ref[...] = jnp.zeros(seen_ref.shape, F32)

    surplus_ties = flag_ref[0, 0] > 0.0

    def pick_with_ties(kk):
        tie = jnp.where(kk == thr_k, 1.0, 0.0)
        seen = seen_ref[...]
        rank = jnp.dot(tie.astype(MXU_DTYPE), upto, preferred_element_type=F32) + jnp.concatenate([seen] * rep_a, axis=1)
        seen_ref[...] = seen + jnp.sum(tie, axis=1, keepdims=True)
        keep_tie = jnp.where(rank <= ties_k, tie, 0.0)
        return jnp.where(kk > thr_k, 0.0, jnp.where(keep_tie > 0.0, 0.0, MASK_BIAS))

    def pick_all_ties(kk):
        return jnp.where(kk >= thr_k, 0.0, MASK_BIAS)

    def att_body(c, carry, pick):
        k0 = pl.multiple_of(c * kca, kca)
        kk = key_ref[:, pl.ds(k0, kca)]
        bias = jnp.where(k0 + lane_a <= t_idx_a, pick(kk), MASK_BIAS)
        bias = jnp.concatenate([bias] * group, axis=0)
        for c2 in range(A_KV_HEADS):
            q = qa_ref[0, c2 * group:(c2 + 1) * group].reshape(group * tq, HEAD_DIM)
            s = _nt_dot(q, ka_ref[0, c2, pl.ds(k0, kca), :]) + bias
            m_prev = m_ref[c2]
            m_next = jnp.maximum(m_prev, jnp.max(s, axis=1, keepdims=True))
            p = jnp.exp2(s - jnp.concatenate([m_next] * rep_a, axis=1))
            alpha = jnp.exp2(m_prev - m_next)
            l_ref[c2] = alpha * l_ref[c2] + jnp.sum(p, axis=1, keepdims=True)
            m_ref[c2] = m_next
            acc_ref[c2] = alpha * acc_ref[c2] + jnp.dot(
                p.astype(MXU_DTYPE), va_ref[0, c2, pl.ds(k0, kca), :], preferred_element_type=F32)
        return carry

    def walk(pick):
        return lax.fori_loop(0, nchw, functools.partial(att_body, pick=pick), 0)

    lax.cond(surplus_ties, lambda: walk(pick_with_ties), lambda: walk(pick_all_ties))
    heads_per_half = HALF_TILE // HEAD_DIM
    for c2 in range(A_KV_HEADS):
        for g in range(group):
            h = c2 * group + g
            rows = slice(g * tq, (g + 1) * tq)
            out = acc_ref[c2, rows, :] / l_ref[c2, rows, :]
            z_ref = (za0_ref, za1_ref)[h // heads_per_half]
            za = z_ref[:, (h % heads_per_half) * HEAD_DIM:(h % heads_per_half + 1) * HEAD_DIM]
            o_ref[:, h * HEAD_DIM:(h + 1) * HEAD_DIM] = (out * _silu(za)).astype(o_ref.dtype)


def _dsa(qi, ki, wi, qa, ka, va, proj, batch, seq, top_k, tq, kc):
    nqt = seq // tq
    n = batch * seq
    row = lambda b, i: b * nqt + i
    group = A_HEADS // A_KV_HEADS
    kern = functools.partial(_dsa_kernel, top_k=top_k, tq=tq, kc=kc)
    stat = pltpu.VMEM((A_KV_HEADS, group * tq, HEAD_DIM), F32)
    return pl.pallas_call(
        kern,
        grid=(batch, nqt),
        in_specs=[
            pl.BlockSpec((1, IDX_HEADS, tq, HEAD_DIM), lambda b, i: (b, 0, i, 0)),
            pl.BlockSpec((1, seq, HEAD_DIM), lambda b, i: (b, 0, 0)),
            pl.BlockSpec((tq, LANES), lambda b, i: (row(b, i), 0)),
            pl.BlockSpec((1, A_HEADS, tq, HEAD_DIM), lambda b, i: (b, 0, i, 0)),
            pl.BlockSpec((1, A_KV_HEADS, seq, HEAD_DIM), lambda b, i: (b, 0, 0, 0)),
            pl.BlockSpec((1, A_KV_HEADS, seq, HEAD_DIM), lambda b, i: (b, 0, 0, 0)),
            pl.BlockSpec((tq, HALF_TILE), lambda b, i: (row(b, i), COL_ZA // HALF_TILE)),
            pl.BlockSpec((tq, HALF_TILE), lambda b, i: (row(b, i), COL_ZA // HALF_TILE + 1)),
        ],
        out_specs=pl.BlockSpec((tq, A_WIDTH), lambda b, i: (row(b, i), 0)),
        out_shape=jax.ShapeDtypeStruct((n, A_WIDTH), MXU_DTYPE),
        scratch_shapes=[
            pltpu.VMEM((tq, seq), jnp.int32),
            pltpu.VMEM((seq, tq), jnp.int16),
            pltpu.VMEM((seq, tq), jnp.int16),
            pltpu.VMEM((IDX_HEADS, tq, kc), F32),
            pltpu.VMEM((tq, LANES), jnp.int32),
            pltpu.VMEM((tq, LANES), F32),
            pltpu.VMEM((tq, LANES), F32),
            pltpu.VMEM((SUBLANES, LANES), F32),
            stat, stat, stat,
        ],
        compiler_params=pltpu.CompilerParams(
            dimension_semantics=("arbitrary", "arbitrary"), vmem_limit_bytes=VMEM_LIMIT),
        name="dsa",
    )(qi, ki, wi, qa, ka, va, proj, proj)


SB_SUB = 128
SB_DEAD_LOG2 = -160.0


def _sb_chunk(q, kch, vch, tri2, rests, row0, key0, kc):
    nsub = q.shape[0] // SB_SUB
    z = _nt_dot(q, kch)
    his, los, lbs, sums, masks = [], [], [], [], []
    for s in range(nsub):
        zs = z[s * SB_SUB:(s + 1) * SB_SUB]
        m0 = jnp.minimum(zs, 0.0)
        t1 = m0 - zs
        sp = jnp.log2(1.0 + jnp.exp2(m0 + t1))
        log_beta = m0 - sp
        log_keep = t1 - sp
        mask = None
        first_row = row0 + s * SB_SUB
        if key0 is not None and key0 + kc - 1 >= first_row:
            mask = (key0 + lax.broadcasted_iota(jnp.int32, (SB_SUB, kc), 1)
                    < first_row + lax.broadcasted_iota(jnp.int32, (SB_SUB, kc), 0))
            log_keep = jnp.where(mask, log_keep, 0.0)
        hi = log_keep.astype(MXU_DTYPE)
        his.append(hi)
        los.append((log_keep - hi.astype(F32)).astype(MXU_DTYPE))
        lbs.append(log_beta)
        masks.append(mask)
        sums.append(jnp.sum(log_keep, axis=1, keepdims=True))
    hl = jnp.concatenate([jnp.concatenate(his, axis=0), jnp.concatenate(los, axis=0)], axis=1)
    after = jnp.dot(hl, tri2, preferred_element_type=F32)
    es = []
    for s in range(nsub):
        e = jnp.exp2(lbs[s] + after[s * SB_SUB:(s + 1) * SB_SUB] + rests[s])
        if masks[s] is not None:
            e = jnp.where(masks[s], e, 0.0)
        es.append(e.astype(MXU_DTYPE))
    pv = jnp.dot(jnp.concatenate(es, axis=0), vch, preferred_element_type=F32)
    return [r + sm for r, sm in zip(rests, sums)], pv


SB_HEADS_PER_STEP = 2


def _sb_kernel(q_ref, k_ref, v_ref, z_ref, o_ref, *, tq, kc, hp):
    i = pl.program_id(2)
    nsub = tq // SB_SUB
    ndiag = tq // kc
    tri = jnp.where(lax.broadcasted_iota(jnp.int32, (kc, kc), 0) > lax.broadcasted_iota(jnp.int32, (kc, kc), 1),
                    1.0, 0.0).astype(MXU_DTYPE)
    tri2 = jnp.concatenate([tri, tri], axis=0)

    def kv(h, k0):
        k0 = pl.multiple_of(k0, kc)
        return k_ref[0, h, pl.ds(k0, kc), :], v_ref[0, h, pl.ds(k0, kc), :]

    qs, state = [], []
    for h in range(hp):
        q = q_ref[0, h]
        rests = [jnp.zeros((SB_SUB, 1), F32) for _ in range(nsub)]
        acc = jnp.zeros((tq, HEAD_DIM), F32)
        for d in reversed(range(ndiag)):
            row0 = d * kc
            s0 = row0 // SB_SUB
            kch, vch = kv(h, i * tq + d * kc)
            new, pv = _sb_chunk(q[row0:], kch, vch, tri2, rests[s0:], row0, d * kc, kc)
            rests = rests[:s0] + new
            acc = jnp.concatenate([acc[:row0], acc[row0:] + pv], axis=0) if row0 else acc + pv
        qs.append(q)
        state.append((tuple(rests), acc))

    nfull = i * ndiag

    def alive(state):
        top = None
        for rests, _ in state:
            for r in rests:
                top = r if top is None else jnp.maximum(top, r)
        return (jnp.max(top) > SB_DEAD_LOG2).astype(jnp.int32)

    def cond(carry):
        return jnp.logical_and(carry[0] < nfull, carry[1] > 0)

    def body(carry):
        j, _, state = carry
        out = []
        for h in range(hp):
            rests, acc = state[h]
            kch, vch = kv(h, (nfull - 1 - j) * kc)
            rests, pv = _sb_chunk(qs[h], kch, vch, tri2, list(rests), 0, None, kc)
            out.append((tuple(rests), acc + pv))
        out = tuple(out)
        return j + 1, alive(out), out

    state = lax.while_loop(cond, body, (jnp.int32(0), alive(state), tuple(state)))[2]
    for h in range(hp):
        sl = slice(h * HEAD_DIM, (h + 1) * HEAD_DIM)
        o_ref[:, sl] = (state[h][1] * _silu(z_ref[:, sl])).astype(o_ref.dtype)


def _sb(qc, kc_, vc, proj, batch, seq, tq, kc, hp):
    nqb = seq // tq
    n = batch * seq
    col_zc = COL_ZC // (hp * HEAD_DIM)
    kern = functools.partial(_sb_kernel, tq=tq, kc=kc, hp=hp)
    return pl.pallas_call(
        kern,
        grid=(batch, C_HEADS // hp, nqb),
        in_specs=[
            pl.BlockSpec((1, hp, tq, HEAD_DIM), lambda b, h, i: (b, h, i, 0)),
            pl.BlockSpec((1, hp, seq, HEAD_DIM), lambda b, h, i: (b, h, 0, 0)),
            pl.BlockSpec((1, hp, seq, HEAD_DIM), lambda b, h, i: (b, h, 0, 0)),
            pl.BlockSpec((tq, hp * HEAD_DIM), lambda b, h, i: (b * nqb + i, col_zc + h)),
        ],
        out_specs=pl.BlockSpec((tq, hp * HEAD_DIM), lambda b, h, i: (b * nqb + i, h)),
        out_shape=jax.ShapeDtypeStruct((n, C_WIDTH), MXU_DTYPE),
        compiler_params=pltpu.CompilerParams(
            dimension_semantics=("arbitrary", "arbitrary", "arbitrary"), vmem_limit_bytes=VMEM_LIMIT),
        name="sb",
    )(qc, kc_, vc, proj)


CONV_HIST = 32


def _conv_kernel(ua_ref, ug_ref, zb_ref, cw_ref, cb_ref, lw_ref, lb_ref, o_ref, ysh, *, ts, width, rc):
    ch = o_ref.shape[1]

    @pl.when(pl.program_id(1) == 0)
    def _():
        ysh[0, 0:CONV_HIST, :] = jnp.zeros((CONV_HIST, ch), F32)

    @pl.when(pl.program_id(1) > 0)
    def _():
        ysh[0, 0:CONV_HIST, :] = ysh[0, ts:ts + CONV_HIST, :]

    ysh[0, CONV_HIST:CONV_HIST + ts, :] = ua_ref[...] * jax.nn.sigmoid(ug_ref[...])
    span = CONV_HIST + ts - SUBLANES
    for r in range(1, SUBLANES):
        ysh[r, 0:span, :] = ysh[0, r:r + span, :]

    def chunk(c, carry):
        r0 = pl.multiple_of(c * rc, rc)
        acc = jnp.broadcast_to(cb_ref[...], (rc, ch))
        for j in range(width):
            off = CONV_HIST - (width - 1) + j
            r = off % SUBLANES
            acc = acc + cw_ref[j:j + 1, :] * ysh[r, pl.ds(r0 + (off - r), rc), :]
        mu = jnp.mean(acc, axis=1, keepdims=True)
        d = acc - mu
        var = jnp.mean(d * d, axis=1, keepdims=True)
        y = d * lax.rsqrt(var + EPS) * lw_ref[...] + lb_ref[...]
        o_ref[pl.ds(r0, rc), :] = (_silu(y) * _silu(zb_ref[pl.ds(r0, rc), :])).astype(o_ref.dtype)
        return carry

    lax.fori_loop(0, ts // rc, chunk, 0)


def _conv(proj, conv_w, conv_b, ln_w, ln_b, batch, seq, ts, rc):
    n = batch * seq
    width, ch = conv_w.shape
    nst = seq // ts
    vec = pl.BlockSpec((1, ch), lambda b, s: (0, 0))
    kern = functools.partial(_conv_kernel, ts=ts, width=width, rc=rc)
    return pl.pallas_call(
        kern,
        grid=(batch, nst),
        in_specs=[
            pl.BlockSpec((ts, ch), lambda b, s: (b * nst + s, COL_UB // ch)),
            pl.BlockSpec((ts, ch), lambda b, s: (b * nst + s, COL_UB // ch + 1)),
            pl.BlockSpec((ts, ch), lambda b, s: (b * nst + s, COL_ZB // ch)),
            pl.BlockSpec((width, ch), lambda b, s: (0, 0)),
            vec, vec, vec,
        ],
        out_specs=pl.BlockSpec((ts, ch), lambda b, s: (b * nst + s, 0)),
        out_shape=jax.ShapeDtypeStruct((n, ch), MXU_DTYPE),
        scratch_shapes=[pltpu.VMEM((SUBLANES, CONV_HIST + ts, ch), F32)],
        compiler_params=pltpu.CompilerParams(
            dimension_semantics=("arbitrary", "arbitrary"), vmem_limit_bytes=VMEM_LIMIT),
        name="conv",
    )(proj, proj, proj, conv_w, conv_b.reshape(1, ch), ln_w.reshape(1, ch), ln_b.reshape(1, ch))


def _merge_kernel(ya_ref, yb_ref, yc_ref, g0_ref, g1_ref, g2_ref, wa_ref, wb_ref, wc_ref, o_ref):
    ya = jnp.dot(ya_ref[...], wa_ref[...], preferred_element_type=F32)
    yb = jnp.dot(yb_ref[...], wb_ref[...], preferred_element_type=F32)
    yc = jnp.dot(yc_ref[...], wc_ref[...], preferred_element_type=F32)
    merged = (jax.nn.sigmoid(g0_ref[...]) * ya + jax.nn.sigmoid(g1_ref[...]) * yb
              + jax.nn.sigmoid(g2_ref[...]) * yc)
    o_ref[...] = merged.astype(o_ref.dtype)


def _merge(ya, yb, yc, proj, wa, wb, wc, col_gates, tm):
    n = ya.shape[0]
    d = wa.shape[1]
    act = lambda w: pl.BlockSpec((tm, w), lambda i: (i, 0))
    gate = lambda k: pl.BlockSpec((tm, d), lambda i, k=k: (i, col_gates // d + k))
    wspec = lambda w: pl.BlockSpec(w.shape, lambda i: (0, 0))
    return pl.pallas_call(
        _merge_kernel,
        grid=(n // tm,),
        in_specs=[act(ya.shape[1]), act(yb.shape[1]), act(yc.shape[1]), gate(0), gate(1), gate(2),
                  wspec(wa), wspec(wb), wspec(wc)],
        out_specs=pl.BlockSpec((tm, d), lambda i: (i, 0)),
        out_shape=jax.ShapeDtypeStruct((n, d), MXU_DTYPE),
        compiler_params=pltpu.CompilerParams(
            dimension_semantics=("arbitrary",), vmem_limit_bytes=VMEM_LIMIT),
        name="merge",
    )(ya, yb, yc, proj, proj, proj, wa, wb, wc)


def _outproj_kernel(m_ref, x_ref, w_ref, fw_ref, o_ref, *, final_norm):
    y = x_ref[...] + jnp.dot(m_ref[...], w_ref[...], preferred_element_type=F32)
    if final_norm:
        ms = jnp.mean(y * y, axis=-1, keepdims=True)
        y = y * lax.rsqrt(ms + EPS) * fw_ref[...]
    o_ref[...] = y


def _outproj(merged, x2d, w_out, final_w, final_norm, tm):
    n, d = x2d.shape
    kern = functools.partial(_outproj_kernel, final_norm=final_norm)
    return pl.pallas_call(
        kern,
        grid=(n // tm,),
        in_specs=[
            pl.BlockSpec((tm, d), lambda i: (i, 0)),
            pl.BlockSpec((tm, d), lambda i: (i, 0)),
            pl.BlockSpec((d, d), lambda i: (0, 0)),
            pl.BlockSpec((1, d), lambda i: (0, 0)),
        ],
        out_specs=pl.BlockSpec((tm, d), lambda i: (i, 0)),
        out_shape=jax.ShapeDtypeStruct((n, d), F32),
        compiler_params=pltpu.CompilerParams(
            dimension_semantics=("arbitrary",), vmem_limit_bytes=VMEM_LIMIT),
        name="outproj",
    )(merged, x2d, w_out, final_w.reshape(1, d))


def _wprep_kernel(a_ref, b_ref, o_ref, *, used):
    j = pl.program_id(1)
    nfirst = COL_UB // COL_TILE
    col = j * COL_TILE + lax.broadcasted_iota(jnp.int32, o_ref.shape, 1)

    @pl.when(j < nfirst)
    def _():
        o_ref[...] = jnp.where(col < FIRST_RUN, a_ref[...].T, 0.0).astype(o_ref.dtype)

    @pl.when(j >= nfirst)
    def _():
        shift = FIRST_RUN - (COL_UB - COL_TILE)
        win = jnp.concatenate([a_ref[...], b_ref[...]], axis=0)[shift:shift + COL_TILE]
        o_ref[...] = jnp.where(col < used, win.T, 0.0).astype(o_ref.dtype)


def _prepare_w_in(w_in, d, conv_ch, kb):
    assert conv_ch == CONV_CH and w_in.shape[-1] == FIRST_RUN + (COL_GATES - COL_UB) + N_BRANCH * d
    assert (FIRST_RUN - COL_UB) % SUBLANES == 0
    depth = w_in.shape[0]
    used = COL_GATES + N_BRANCH * d
    np_ = -(-used // COL_TILE) * COL_TILE
    nfirst = COL_UB // COL_TILE
    w_t = jnp.swapaxes(w_in, 1, 2)
    src = lambda back: pl.BlockSpec(
        (None, COL_TILE, kb), lambda l, j, r: (l, jnp.where(j < nfirst, j, j - back), r))
    return pl.pallas_call(
        functools.partial(_wprep_kernel, used=used),
        grid=(depth, np_ // COL_TILE, d // kb),
        in_specs=[src(1), src(0)],
        out_specs=pl.BlockSpec((None, kb, COL_TILE), lambda l, j, r: (l, r, j)),
        out_shape=jax.ShapeDtypeStruct((depth, d, np_), MXU_DTYPE),
        compiler_params=pltpu.CompilerParams(
            dimension_semantics=("arbitrary", "arbitrary", "arbitrary"), vmem_limit_bytes=VMEM_LIMIT),
        name="wprep",
    )(w_t, w_t)


def _rope_tables(seq):
    inv = 1.0 / (ROPE_THETA ** (jnp.arange(0, HEAD_DIM, 2, dtype=F32) / HEAD_DIM))
    ang = jnp.arange(seq, dtype=F32)[:, None] * inv[None, :]
    cos, sin = jnp.cos(ang), jnp.sin(ang)
    return jnp.concatenate([cos, cos], axis=1), jnp.concatenate([-sin, sin], axis=1)


def kernel(x, norm_w, w_in, conv_w, conv_b, conv_ln_w, conv_ln_b, w_a_out, w_b_out, w_c_out, w_out, final_norm_w):
    batch, seq, d = x.shape
    depth = norm_w.shape[0]
    conv_ch = conv_b.shape[-1]
    n = batch * seq
    assert seq % BLOCK_Q == 0 and conv_ch == COL_TILE and conv_w.shape[1] - 1 <= CONV_HIST
    top_k = min(IDX_TOPK, seq // 4)
    kc = 256
    assert seq % kc == 0 and COL_GATES % d == 0

    w_prep = _prepare_w_in(w_in, d, conv_ch, min(512, d))
    wa, wb, wc, wo = (w.astype(MXU_DTYPE) for w in (w_a_out, w_b_out, w_c_out, w_out))
    cw = conv_w.reshape(depth, conv_w.shape[1], conv_ch)
    cos_t, sin_t = _rope_tables(seq)

    tm_in = min(1024, n)
    ts_prep = min(256, seq)
    ts_conv = min(512, seq)
    tq_dsa = min(256, seq)
    tq_sb = min(512, seq)
    assert tq_dsa % kc == 0 and tq_sb % kc == 0 and seq % tq_sb == 0 and seq % (ATT_MULT * kc) == 0

    x2d = x.reshape(n, d)
    for layer in range(depth):
        proj = _inproj(x2d, norm_w[layer], w_prep, layer, tm_in, COL_TILE)
        qi, qa, qc, kcs, vcs, ka, va, ki, wi = _prep(proj, cos_t, sin_t, batch, seq, ts_prep)
        ya = _dsa(qi, ki, wi, qa, ka, va, proj, batch, seq, top_k, tq_dsa, kc)
        yc = _sb(qc, kcs, vcs, proj, batch, seq, tq_sb, kc, SB_HEADS_PER_STEP)
        yb = _conv(proj, cw[layer], conv_b[layer], conv_ln_w[layer], conv_ln_b[layer], batch, seq, ts_conv, 64)
        merged = _merge(ya, yb, yc, proj, wa[layer], wb[layer], wc[layer], COL_GATES, min(256, n))
        x2d = _outproj(merged, x2d, wo[layer], final_norm_w, layer == depth - 1, min(512, n))
    return x2d.reshape(batch, seq, d)
```
